```python
import jax, jax.numpy as jnp
from jax import lax
import numpy as np

D_MODEL = 1024
BATCH = 4
SEQ = 8192
DEPTH = 1
DEC_BATCH = 32
DEC_SEQ = 16
PAST_LEN = 2048

CHUNK = 64
SGU_CHUNK = 128
SB_BLOCK = 128
MIX_WIDTH = D_MODEL
SGU_WIDTH = MIX_WIDTH // 2
SGU_HEADS = 4
SGU_HEAD_DIM = SGU_WIDTH // SGU_HEADS
SB_WIDTH = MIX_WIDTH - SGU_WIDTH
SB_HEADS = 8
SB_HEAD_DIM = SB_WIDTH // SB_HEADS
D_FF = ((8 * D_MODEL // 3 + 127) // 128) * 128
CONV_WIDTH = 3
IN_WIDTH = 2 * SGU_WIDTH + 3 * SB_WIDTH
EPS = 1e-6

kernel_name = "hybrid_sgu_stickbreak_convffn_step"


def _rms_norm(x, g):
    xf = x.astype(jnp.float32)
    y = xf * lax.rsqrt(jnp.mean(xf * xf, axis=-1, keepdims=True) + EPS)
    return (y * g.astype(jnp.float32)).astype(x.dtype)


def _layer_norm(x, g, b):
    xf = x.astype(jnp.float32)
    mu = jnp.mean(xf, axis=-1, keepdims=True)
    xc = xf - mu
    y = xc * lax.rsqrt(jnp.mean(xc * xc, axis=-1, keepdims=True) + EPS)
    return (y * g.astype(jnp.float32) + b.astype(jnp.float32)).astype(x.dtype)


def _spatial_gating(u, v, w_s, b_s):
    bsz, t, _ = v.shape
    L = min(t, SGU_CHUNK)
    nc = t // L
    mask = jnp.tril(jnp.ones((L, L), dtype=bool))
    w = jnp.where(mask[None], w_s[:, :L, :L], 0).astype(v.dtype)
    vh = v.reshape(bsz, nc, L, SGU_HEADS, SGU_HEAD_DIM)
    bias = b_s[:, :L].T.astype(v.dtype)[None, None, :, :, None]
    mixed = jnp.einsum('hts,bcshe->bcthe', w, vh) + bias
    return u * mixed.reshape(bsz, t, SGU_WIDTH)


def _stick_breaking_block(q, k, v, q_pos, k_pos):
    z = jnp.einsum('bqhd,bkhd->bhqk', q, k).astype(jnp.float32) * (SB_HEAD_DIM ** -0.5)
    reach = k_pos[None, :] < q_pos[:, None]
    log_fail = jnp.where(reach, jax.nn.log_sigmoid(-z), 0.0)
    incl = lax.cumsum(log_fail, axis=3, reverse=True)
    excl = jnp.concatenate([incl[..., 1:], jnp.zeros_like(incl[..., :1])], axis=-1)
    w = jnp.where(reach, jnp.exp(jax.nn.log_sigmoid(z) + excl), 0.0)
    return jnp.einsum('bhqk,bkhd->bqhd', w.astype(v.dtype), v)


def _stick_breaking(q, k, v, q_pos, k_pos):
    bsz, t, h, d = q.shape
    if t <= SB_BLOCK:
        return _stick_breaking_block(q, k, v, q_pos, k_pos)
    nb = t // SB_BLOCK
    qb = jnp.moveaxis(q.reshape(bsz, nb, SB_BLOCK, h, d), 1, 0)
    pb = q_pos.reshape(nb, SB_BLOCK)
    ob = lax.map(lambda a: _stick_breaking_block(a[0], k, v, a[1], k_pos), (qb, pb))
    return jnp.moveaxis(ob, 0, 1).reshape(bsz, t, h, d)


def _causal_dwconv(h, hist, w, b):
    t = h.shape[1]
    full = jnp.concatenate([hist, h], axis=1)
    y = full[:, 0:t] * w[0] + full[:, 1:t + 1] * w[1] + full[:, 2:t + 2] * w[2] + b
    return y, full[:, -(CONV_WIDTH - 1):]


def _layer(x, past_k, past_v, conv_hist, w_in, g_pre_mix, ln_v_g, ln_v_b, w_spatial, b_spatial,
           g_out_a, g_out_b, w_out, g_post_mix, g_pre_ffn, w_up, conv_w, conv_b, w_down, g_post_ffn):
    bsz, t, _ = x.shape
    past = past_k.shape[1]
    h = _rms_norm(x, g_pre_mix)
    p = h @ w_in
    za = jax.nn.gelu(p[..., :2 * SGU_WIDTH])
    o = 2 * SGU_WIDTH
    q = p[..., o:o + SB_WIDTH].reshape(bsz, t, SB_HEADS, SB_HEAD_DIM)
    k = p[..., o + SB_WIDTH:o + 2 * SB_WIDTH].reshape(bsz, t, SB_HEADS, SB_HEAD_DIM)
    v = p[..., o + 2 * SB_WIDTH:o + 3 * SB_WIDTH].reshape(bsz, t, SB_HEADS, SB_HEAD_DIM)
    u_a = za[..., :SGU_WIDTH]
    v_a = _layer_norm(za[..., SGU_WIDTH:], ln_v_g, ln_v_b)
    out_a = _spatial_gating(u_a, v_a, w_spatial, b_spatial)
    k_all = jnp.concatenate([past_k, k], axis=1)
    v_all = jnp.concatenate([past_v, v], axis=1)
    q_pos = past + jnp.arange(t)
    k_pos = jnp.arange(past + t)
    out_b = _stick_breaking(q, k_all, v_all, q_pos, k_pos).reshape(bsz, t, SB_WIDTH)
    mix = jnp.concatenate([_rms_norm(out_a, g_out_a), _rms_norm(out_b, g_out_b)], axis=-1) @ w_out
    x = x + _rms_norm(mix, g_post_mix)
    up = _rms_norm(x, g_pre_ffn) @ w_up
    up_c, new_hist = _causal_dwconv(up, conv_hist, conv_w, conv_b)
    f = (jax.nn.gelu(up_c[..., :D_FF]) * up_c[..., D_FF:]) @ w_down
    x = x + _rms_norm(f, g_post_ffn)
    return x, k, v, v_a, new_hist


def setup_inputs(seed: int = 0) -> dict:
    key = jax.random.key(seed)
    ks = jax.random.split(key, 24)
    n = lambda i, shape, s=1.0: jax.random.normal(ks[i], shape, jnp.float32) * s
    gain = lambda i, width: 1.0 + n(i, (DEPTH, width), 0.1)
    return {
        "x_prompt": n(0, (BATCH, SEQ, D_MODEL)),
        "x_sample": n(1, (DEC_BATCH, DEC_SEQ, D_MODEL)),
        "cache_sb_k": n(2, (DEPTH, DEC_BATCH, PAST_LEN, SB_HEADS, SB_HEAD_DIM)),
        "cache_sb_v": n(3, (DEPTH, DEC_BATCH, PAST_LEN, SB_HEADS, SB_HEAD_DIM)),
        "cache_ffn_conv": n(4, (DEPTH, DEC_BATCH, CONV_WIDTH - 1, 2 * D_FF)),
        "w_in": n(5, (DEPTH, D_MODEL, IN_WIDTH), D_MODEL ** -0.5),
        "g_pre_mix": gain(6, D_MODEL),
        "ln_v_g": gain(7, SGU_WIDTH),
        "ln_v_b": n(8, (DEPTH, SGU_WIDTH), 0.01),
        "w_spatial": n(9, (DEPTH, SGU_HEADS, SGU_CHUNK, SGU_CHUNK), SGU_CHUNK ** -0.5),
        "b_spatial": 1.0 + n(10, (DEPTH, SGU_HEADS, SGU_CHUNK), 0.1),
        "g_out_a": gain(11, SGU_WIDTH),
        "g_out_b": gain(12, SB_WIDTH),
        "w_out": n(13, (DEPTH, MIX_WIDTH, D_MODEL), MIX_WIDTH ** -0.5),
        "g_post_mix": gain(14, D_MODEL),
        "g_pre_ffn": gain(15, D_MODEL),
        "w_up": n(16, (DEPTH, D_MODEL, 2 * D_FF), D_MODEL ** -0.5),
        "conv_w": n(17, (DEPTH, CONV_WIDTH, 2 * D_FF), CONV_WIDTH ** -0.5),
        "conv_b": n(18, (DEPTH, 2 * D_FF), 0.01),
        "w_down": n(19, (DEPTH, D_FF, D_MODEL), D_FF ** -0.5),
        "g_post_ffn": gain(20, D_MODEL),
    }


def reference(x_prompt, x_sample, cache_sb_k, cache_sb_v, cache_ffn_conv, w_in, g_pre_mix, ln_v_g,
              ln_v_b, w_spatial, b_spatial, g_out_a, g_out_b, w_out, g_post_mix, g_pre_ffn, w_up,
              conv_w, conv_b, w_down, g_post_ffn):
    yp, ys = x_prompt, x_sample
    bp = x_prompt.shape[0]
    empty_kv = jnp.zeros((bp, 0, SB_HEADS, SB_HEAD_DIM), x_prompt.dtype)
    zero_hist = jnp.zeros((bp, CONV_WIDTH - 1, 2 * D_FF), x_prompt.dtype)
    kp_l, vp_l, cp_l, ks_l, vs_l, vas_l, cs_l = [], [], [], [], [], [], []
    for l in range(DEPTH):
        params = (w_in[l], g_pre_mix[l], ln_v_g[l], ln_v_b[l], w_spatial[l], b_spatial[l],
                  g_out_a[l], g_out_b[l], w_out[l], g_post_mix[l], g_pre_ffn[l], w_up[l],
                  conv_w[l], conv_b[l], w_down[l], g_post_ffn[l])
        yp, kp, vp, _, cp = _layer(yp, empty_kv, empty_kv, zero_hist, *params)
        ys, ksm, vsm, vas, cs = _layer(ys, cache_sb_k[l], cache_sb_v[l], cache_ffn_conv[l], *params)
        kp_l.append(kp); vp_l.append(vp); cp_l.append(cp)
        ks_l.append(ksm); vs_l.append(vsm); vas_l.append(vas); cs_l.append(cs)
    return (yp, ys, jnp.stack(kp_l), jnp.stack(vp_l), jnp.stack(ks_l), jnp.stack(vs_l),
            jnp.stack(vas_l), jnp.stack(cp_l), jnp.stack(cs_l))
```

```python
import functools
import math

import jax
import jax.numpy as jnp
from jax import lax
from jax.experimental import pallas as pl
from jax.experimental.pallas import tpu as pltpu

D_MODEL = 1024
SGU_WIDTH = 512
SGU_HEADS = 4
SGU_HEAD_DIM = 128
SGU_CHUNK = 128
SB_WIDTH = 512
SB_HEADS = 8
SB_HEAD_DIM = 64
D_FF = 2816
CONV_WIDTH = 3
EPS = 1e-6

V7X_LANES = 128
V7X_SUBLANES = 8
V7X_MXU_DIM = 256
V7X_VMEM_LIMIT_BYTES = 56 * 1024 * 1024

HEADS_PER_BLOCK = V7X_LANES // SB_HEAD_DIM
KEY_SUB = V7X_MXU_DIM
KEY_SUBS_PER_TILE = 2
KEY_TILE = KEY_SUB * KEY_SUBS_PER_TILE
FF_CHUNK = V7X_MXU_DIM
LOG2E = 1.0 / math.log(2.0)


def _rms(x, g):
    return x * lax.rsqrt(jnp.mean(x * x, axis=-1, keepdims=True) + EPS) * g


def _gelu(x):
    return jax.nn.gelu(x)


def _dot(a, b):
    return jnp.dot(a, b, preferred_element_type=jnp.float32)


def _in_proj_kernel(x_ref, g_pre_ref, w_in_ref, ln_g_ref, ln_b_ref, wsp_ref, bsp_ref, g_a_ref,
                    va_ref, an_ref, q_ref, kb_ref, vb_ref, k_ref, v_ref, outa_ref, *, q_scale):
    tm = x_ref.shape[0]
    h = _rms(x_ref[...], g_pre_ref[...]).astype(jnp.bfloat16)

    def proj(col):
        return _dot(h, w_in_ref[:, col:col + SGU_WIDTH])

    u = _gelu(proj(0))
    gv = _gelu(proj(SGU_WIDTH))
    mu = jnp.mean(gv, axis=-1, keepdims=True)
    gc = gv - mu
    va = gc * lax.rsqrt(jnp.mean(gc * gc, axis=-1, keepdims=True) + EPS) * ln_g_ref[...] + ln_b_ref[...]
    va_ref[...] = va
    va_b = va.astype(jnp.bfloat16)
    for c in range(tm // SGU_CHUNK):
        r0 = c * SGU_CHUNK
        for hd in range(SGU_HEADS):
            c0 = hd * SGU_HEAD_DIM
            mixed = _dot(wsp_ref[hd], va_b[r0:r0 + SGU_CHUNK, c0:c0 + SGU_HEAD_DIM])
            mixed = mixed + bsp_ref[:, c0:c0 + SGU_HEAD_DIM]
            outa_ref[r0:r0 + SGU_CHUNK, c0:c0 + SGU_HEAD_DIM] = u[r0:r0 + SGU_CHUNK, c0:c0 + SGU_HEAD_DIM] * mixed
    an_ref[...] = _rms(outa_ref[...], g_a_ref[...]).astype(jnp.bfloat16)

    o = 2 * SGU_WIDTH
    q_ref[...] = (proj(o) * q_scale).astype(jnp.bfloat16)
    k = proj(o + SB_WIDTH)
    k_ref[...] = k
    kb_ref[...] = k.astype(jnp.bfloat16)
    v = proj(o + 2 * SB_WIDTH)
    v_ref[...] = v
    vb_ref[...] = v.astype(jnp.bfloat16)


def _in_proj(x, g_pre, w_in_b, ln_g, ln_b, wsp, bsp, g_a, *, tm):
    m = x.shape[0]
    in_width = w_in_b.shape[1]
    row = lambda i: (i, 0)
    fixed2 = lambda i: (0, 0)
    fixed3 = lambda i: (0, 0, 0)
    f32 = jnp.float32
    bf16 = jnp.bfloat16
    wide = lambda dt: jax.ShapeDtypeStruct((m, SGU_WIDTH), dt)
    blk = pl.BlockSpec((tm, SGU_WIDTH), row)
    return pl.pallas_call(
        functools.partial(_in_proj_kernel, q_scale=SB_HEAD_DIM ** -0.5 * LOG2E),
        grid=(m // tm,),
        in_specs=[
            pl.BlockSpec((tm, D_MODEL), row),
            pl.BlockSpec((1, D_MODEL), fixed2),
            pl.BlockSpec((D_MODEL, in_width), fixed2),
            pl.BlockSpec((1, SGU_WIDTH), fixed2),
            pl.BlockSpec((1, SGU_WIDTH), fixed2),
            pl.BlockSpec((SGU_HEADS, SGU_CHUNK, SGU_CHUNK), fixed3),
            pl.BlockSpec((SGU_CHUNK, SGU_WIDTH), fixed2),
            pl.BlockSpec((1, SGU_WIDTH), fixed2),
        ],
        out_specs=[blk] * 7,
        out_shape=[wide(f32), wide(bf16), wide(bf16), wide(bf16), wide(bf16), wide(f32), wide(f32)],
        scratch_shapes=[pltpu.VMEM((tm, SGU_WIDTH), f32)],
        compiler_params=pltpu.CompilerParams(
            dimension_semantics=("arbitrary",), vmem_limit_bytes=V7X_VMEM_LIMIT_BYTES),
        name="in_proj",
    )(x, g_pre, w_in_b, ln_g, ln_b, wsp, bsp, g_a)


def _sb_attn_kernel(q_ref, k_ref, v_ref, u_ref, o_ref, acc_ref, carry_ref, *, past, tq):
    i = pl.program_id(2)
    lo = past + i * tq
    hi = lo + tq - 1
    n_full = lo // KEY_TILE
    n_tot = jnp.maximum(hi - 1, 0) // KEY_TILE + 1

    lane = lax.broadcasted_iota(jnp.int32, (tq, V7X_LANES), 1)
    q2 = q_ref[0]
    zero = jnp.zeros_like(q2)
    q_heads = [jnp.where((lane // SB_HEAD_DIM) == hd, q2, zero) for hd in range(HEADS_PER_BLOCK)]
    u_neg = u_ref[...]

    acc_ref[...] = jnp.zeros_like(acc_ref)
    carry_ref[...] = jnp.zeros_like(carry_ref)

    def tile(j, masked):
        for sb in reversed(range(KEY_SUBS_PER_TILE)):
            k0 = pl.multiple_of(j * KEY_TILE + sb * KEY_SUB, KEY_SUB)
            k2 = k_ref[0, pl.ds(k0, KEY_SUB), :]
            v2 = v_ref[0, pl.ds(k0, KEY_SUB), :]
            if masked:
                q_pos = lo + lax.broadcasted_iota(jnp.int32, (tq, KEY_SUB), 0)
                k_pos = k0 + lax.broadcasted_iota(jnp.int32, (tq, KEY_SUB), 1)
                reach = k_pos < q_pos
            for hd in range(HEADS_PER_BLOCK):
                z = lax.dot_general(q_heads[hd], k2, (((1,), (1,)), ((), ())),
                                    preferred_element_type=jnp.float32)
                sp = jnp.maximum(z, 0.0) + jnp.log2(1.0 + jnp.exp2(-jnp.abs(z)))
                if masked:
                    sp = jnp.where(reach, sp, 0.0)
                excl = _dot(sp.astype(jnp.bfloat16), u_neg)
                carry = carry_ref[hd]
                w = jnp.exp2((z - sp) + excl + carry)
                if masked:
                    w = jnp.where(reach, w, 0.0)
                acc_ref[hd] += _dot(w.astype(jnp.bfloat16), v2)
                carry_ref[hd] = carry - jnp.sum(sp, axis=-1, keepdims=True)

    def masked_body(t, c):
        tile(n_tot - 1 - t, True)
        return c

    def full_body(t, c):
        tile(n_full - 1 - t, False)
        return c

    lax.fori_loop(0, n_tot - n_full, masked_body, 0)
    lax.fori_loop(0, n_full, full_body, 0)

    out = acc_ref[0]
    for hd in range(1, HEADS_PER_BLOCK):
        out = jnp.where((lane // SB_HEAD_DIM) == hd, acc_ref[hd], out)
    o_ref[0] = out


def _sb_attn(q, k_all, v_all, u_neg, *, past, tq):
    b, t, _ = q.shape
    tk = k_all.shape[1]
    n_blk = SB_WIDTH // V7X_LANES
    return pl.pallas_call(
        functools.partial(_sb_attn_kernel, past=past, tq=tq),
        grid=(b, n_blk, t // tq),
        in_specs=[
            pl.BlockSpec((1, tq, V7X_LANES), lambda bi, hp, i: (bi, i, hp)),
            pl.BlockSpec((1, tk, V7X_LANES), lambda bi, hp, i: (bi, 0, hp)),
            pl.BlockSpec((1, tk, V7X_LANES), lambda bi, hp, i: (bi, 0, hp)),
            pl.BlockSpec((KEY_SUB, KEY_SUB), lambda bi, hp, i: (0, 0)),
        ],
        out_specs=pl.BlockSpec((1, tq, V7X_LANES), lambda bi, hp, i: (bi, i, hp)),
        out_shape=jax.ShapeDtypeStruct((b, t, SB_WIDTH), jnp.float32),
        scratch_shapes=[
            pltpu.VMEM((HEADS_PER_BLOCK, tq, V7X_LANES), jnp.float32),
            pltpu.VMEM((HEADS_PER_BLOCK, tq, 1), jnp.float32),
        ],
        compiler_params=pltpu.CompilerParams(
            dimension_semantics=("arbitrary", "arbitrary", "arbitrary"),
            vmem_limit_bytes=V7X_VMEM_LIMIT_BYTES),
        name="sb_attn",
    )(q, k_all, v_all, u_neg)


def _mix_ffn_kernel(*refs, seq_tiles, has_hist):
    if has_hist:
        (x_ref, an_ref, ob_ref, h1_ref, h2_ref, g_b_ref, w_out_ref, g_pm_ref, g_pf_ref, w_up_ref,
         cw_ref, cb_ref, w_dn_ref, g_po_ref, y_ref, up_ref, buf_ref) = refs
    else:
        (x_ref, an_ref, ob_ref, g_b_ref, w_out_ref, g_pm_ref, g_pf_ref, w_up_ref,
         cw_ref, cb_ref, w_dn_ref, g_po_ref, y_ref, hist_ref, buf_ref) = refs
    tm = x_ref.shape[0]
    pad = V7X_SUBLANES

    bn = _rms(ob_ref[...], g_b_ref[...]).astype(jnp.bfloat16)
    mix = _dot(an_ref[...], w_out_ref[0:SGU_WIDTH, :]) + _dot(bn, w_out_ref[SGU_WIDTH:, :])
    x1 = x_ref[...] + _rms(mix, g_pm_ref[...])
    h2 = _rms(x1, g_pf_ref[...]).astype(jnp.bfloat16)

    if has_hist:
        t_in_seq = lax.broadcasted_iota(jnp.int32, (tm, FF_CHUNK), 0) % seq_tiles
        buf_ref[0:pad, :] = jnp.zeros((pad, buf_ref.shape[1]), jnp.float32)
    else:
        @pl.when(pl.program_id(0) % seq_tiles == 0)
        def _():
            buf_ref[0:pad, :] = jnp.zeros((pad, buf_ref.shape[1]), jnp.float32)

    def conv(col):
        up = _dot(h2, w_up_ref[:, col:col + FF_CHUNK])
        buf_ref[pad:pad + tm, col:col + FF_CHUNK] = up
        m1 = buf_ref[pad - 1:pad - 1 + tm, col:col + FF_CHUNK]
        m2 = buf_ref[pad - 2:pad - 2 + tm, col:col + FF_CHUNK]
        if has_hist:
            up_ref[:, col:col + FF_CHUNK] = up
            m1 = jnp.where(t_in_seq >= 1, m1, 0.0) + h1_ref[:, col:col + FF_CHUNK]
            m2 = jnp.where(t_in_seq >= 2, m2, 0.0) + h2_ref[:, col:col + FF_CHUNK]
        else:
            hist_ref[0, :, col:col + FF_CHUNK] = up[tm - 2:tm, :]
            buf_ref[pad - 2:pad, col:col + FF_CHUNK] = up[tm - 2:tm, :]
        w = cw_ref[:, col:col + FF_CHUNK]
        return m2 * w[0:1] + m1 * w[1:2] + up * w[2:3] + cb_ref[:, col:col + FF_CHUNK]

    f = jnp.zeros((tm, D_MODEL), jnp.float32)
    for c in range(D_FF // FF_CHUNK):
        col = c * FF_CHUNK
        act = (_gelu(conv(col)) * conv(D_FF + col)).astype(jnp.bfloat16)
        f = f + _dot(act, w_dn_ref[col:col + FF_CHUNK, :])
    y_ref[...] = x1 + _rms(f, g_po_ref[...])


def _mix_ffn(x, an, ob, hist_taps, g_b, w_out_b, g_pm, g_pf, w_up_b, conv_w, conv_b, w_dn_b, g_po,
             *, tm, seq_len):
    m = x.shape[0]
    row = lambda i: (i, 0)
    fixed = lambda i: (0, 0)
    has_hist = hist_taps is not None
    once = dict(pipeline_mode=pl.Buffered(1))
    in_specs = [
        pl.BlockSpec((tm, D_MODEL), row),
        pl.BlockSpec((tm, SGU_WIDTH), row),
        pl.BlockSpec((tm, SB_WIDTH), row),
    ]
    args = [x, an, ob]
    if has_hist:
        assert tm % seq_len == 0
        seq_tiles = seq_len
        in_specs += [pl.BlockSpec((tm, 2 * D_FF), row)] * 2
        args += list(hist_taps)
        out_specs = [pl.BlockSpec((tm, D_MODEL), row), pl.BlockSpec((tm, 2 * D_FF), row)]
        out_shape = [jax.ShapeDtypeStruct((m, D_MODEL), jnp.float32),
                     jax.ShapeDtypeStruct((m, 2 * D_FF), jnp.float32)]
    else:
        assert seq_len % tm == 0
        seq_tiles = seq_len // tm
        out_specs = [pl.BlockSpec((tm, D_MODEL), row),
                     pl.BlockSpec((1, CONV_WIDTH - 1, 2 * D_FF), lambda i: (i // seq_tiles, 0, 0))]
        out_shape = [jax.ShapeDtypeStruct((m, D_MODEL), jnp.float32),
                     jax.ShapeDtypeStruct((m // seq_len, CONV_WIDTH - 1, 2 * D_FF), jnp.float32)]
    in_specs += [
        pl.BlockSpec((1, SB_WIDTH), fixed),
        pl.BlockSpec((D_MODEL, D_MODEL), fixed, **once),
        pl.BlockSpec((1, D_MODEL), fixed),
        pl.BlockSpec((1, D_MODEL), fixed),
        pl.BlockSpec((D_MODEL, 2 * D_FF), fixed, **once),
        pl.BlockSpec((CONV_WIDTH, 2 * D_FF), fixed),
        pl.BlockSpec((1, 2 * D_FF), fixed),
        pl.BlockSpec((D_FF, D_MODEL), fixed, **once),
        pl.BlockSpec((1, D_MODEL), fixed),
    ]
    args += [g_b, w_out_b, g_pm, g_pf, w_up_b, conv_w, conv_b, w_dn_b, g_po]
    return pl.pallas_call(
        functools.partial(_mix_ffn_kernel, seq_tiles=seq_tiles, has_hist=has_hist),
        grid=(m // tm,),
        in_specs=in_specs,
        out_specs=out_specs,
        out_shape=out_shape,
        scratch_shapes=[pltpu.VMEM((tm + V7X_SUBLANES, 2 * D_FF), jnp.float32)],
        compiler_params=pltpu.CompilerParams(
            dimension_semantics=("arbitrary",), vmem_limit_bytes=V7X_VMEM_LIMIT_BYTES),
        name="mix_ffn",
    )(*args)


def _suffix_matrix():
    r = lax.broadcasted_iota(jnp.int32, (KEY_SUB, KEY_SUB), 0)
    c = lax.broadcasted_iota(jnp.int32, (KEY_SUB, KEY_SUB), 1)
    return jnp.where(r > c, -1.0, 0.0).astype(jnp.bfloat16)


def _layer(x, past_k, past_v, conv_hist, p, *, tm_proj, tq, tm_ffn):
    bsz, t, _ = x.shape
    m = bsz * t
    row = lambda a: a.reshape(1, -1)
    bf16 = jnp.bfloat16

    seq = min(t, SGU_CHUNK)
    w_s = jnp.where(jnp.tril(jnp.ones((seq, seq), bool))[None], p["w_spatial"][:, :seq, :seq], 0.0)
    reps = SGU_CHUNK // seq
    wsp = jnp.einsum("ab,hts->hatbs", jnp.eye(reps, dtype=w_s.dtype), w_s)
    wsp = wsp.reshape(SGU_HEADS, SGU_CHUNK, SGU_CHUNK).astype(bf16)
    b_s = jnp.tile(p["b_spatial"][:, :seq].T, (reps, 1))
    bsp = jnp.repeat(b_s, SGU_HEAD_DIM, axis=1)

    va, an, q, kb, vb, k, v = _in_proj(
        x.reshape(m, D_MODEL), row(p["g_pre_mix"]), p["w_in"].astype(bf16), row(p["ln_v_g"]),
        row(p["ln_v_b"]), wsp, bsp, row(p["g_out_a"]), tm=tm_proj)

    past = past_k.shape[1]
    t_keys = -(-(past + t) // KEY_TILE) * KEY_TILE
    fill = jnp.zeros((bsz, t_keys - past - t, SB_WIDTH), bf16)
    cat = lambda old, new: jnp.concatenate(
        [old.reshape(bsz, past, SB_WIDTH).astype(bf16), new.reshape(bsz, t, SB_WIDTH), fill], axis=1)
    ob = _sb_attn(q.reshape(bsz, t, SB_WIDTH), cat(past_k, kb), cat(past_v, vb), _suffix_matrix(),
                  past=past, tq=tq)

    if conv_hist is None:
        hist_taps = None
    else:
        z = jnp.zeros((bsz, t - 1, 2 * D_FF), jnp.float32)
        tap1 = jnp.concatenate([conv_hist[:, 1:], z], axis=1)
        tap2 = jnp.concatenate([conv_hist, z[:, 1:]], axis=1)
        hist_taps = (tap1.reshape(m, 2 * D_FF), tap2.reshape(m, 2 * D_FF))
    y, extra = _mix_ffn(
        x.reshape(m, D_MODEL), an, ob.reshape(m, SB_WIDTH), hist_taps, row(p["g_out_b"]),
        p["w_out"].astype(bf16), row(p["g_post_mix"]), row(p["g_pre_ffn"]), p["w_up"].astype(bf16),
        p["conv_w"], row(p["conv_b"]), p["w_down"].astype(bf16), row(p["g_post_ffn"]),
        tm=tm_ffn, seq_len=t)
    if conv_hist is None:
        new_hist = extra
    else:
        new_hist = extra.reshape(bsz, t, 2 * D_FF)[:, t - (CONV_WIDTH - 1):]
    shape_kv = (bsz, t, SB_HEADS, SB_HEAD_DIM)
    return (y.reshape(bsz, t, D_MODEL), k.reshape(shape_kv), v.reshape(shape_kv),
            va.reshape(bsz, t, SGU_WIDTH), new_hist)


def kernel(x_prompt, x_sample, cache_sb_k, cache_sb_v, cache_ffn_conv, w_in, g_pre_mix, ln_v_g, ln_v_b, w_spatial, b_spatial, g_out_a, g_out_b, w_out, g_post_mix, g_pre_ffn, w_up, conv_w, conv_b, w_down, g_post_ffn):
    names = ("w_in", "g_pre_mix", "ln_v_g", "ln_v_b", "w_spatial", "b_spatial", "g_out_a", "g_out_b",
             "w_out", "g_post_mix", "g_pre_ffn", "w_up", "conv_w", "conv_b", "w_down", "g_post_ffn")
    stacked = (w_in, g_pre_mix, ln_v_g, ln_v_b, w_spatial, b_spatial, g_out_a, g_out_b, w_out,
               g_post_mix, g_pre_ffn, w_up, conv_w, conv_b, w_down, g_post_ffn)
    depth = w_in.shape[0]
    yp, ys = x_prompt, x_sample
    bp = x_prompt.shape[0]
    empty_kv = jnp.zeros((bp, 0, SB_HEADS, SB_HEAD_DIM), x_prompt.dtype)
    outs = [[] for _ in range(7)]
    for l in range(depth):
        p = {n: a[l] for n, a in zip(names, stacked)}
        yp, kp, vp, _, cp = _layer(yp, empty_kv, empty_kv, None, p, tm_proj=512, tq=512, tm_ffn=256)
        ys, ksm, vsm, vas, cs = _layer(ys, cache_sb_k[l], cache_sb_v[l], cache_ffn_conv[l], p,
                                       tm_proj=512, tq=ys.shape[1], tm_ffn=128)
        for lst, a in zip(outs, (kp, vp, ksm, vsm, vas, cp, cs)):
            lst.append(a)
    return (yp, ys) + tuple(jnp.stack(lst) for lst in outs)
```

```python
import functools
import math

import jax
import jax.numpy as jnp
from jax import lax
from jax.experimental import pallas as pl
from jax.experimental.pallas import tpu as pltpu

D_MODEL = 1024
SGU_WIDTH = 512
SGU_HEADS = 4
SGU_HEAD_DIM = 128
SGU_CHUNK = 128
SB_WIDTH = 512
SB_HEADS = 8
SB_HEAD_DIM = 64
D_FF = 2816
CONV_WIDTH = 3
EPS = 1e-6

V7X_LANES = 128
V7X_SUBLANES = 8
V7X_MXU_DIM = 256
V7X_VMEM_LIMIT_BYTES = 56 * 1024 * 1024

HEADS_PER_BLOCK = V7X_LANES // SB_HEAD_DIM
KEY_SUB = V7X_MXU_DIM
KEY_SUBS_PER_TILE = 4
KEY_TILE = KEY_SUB * KEY_SUBS_PER_TILE
FF_CHUNK = V7X_MXU_DIM
UNREACHABLE = -1e30
GELU_C1 = math.sqrt(2.0 / math.pi)
GELU_C2 = 0.044715 * GELU_C1


def _rms(x, g):
    return x * lax.rsqrt(jnp.mean(x * x, axis=-1, keepdims=True) + EPS) * g


def _gelu(x):
    return jax.nn.gelu(x)


def _dot(a, b):
    return jnp.dot(a, b, preferred_element_type=jnp.float32)


def _in_proj_kernel(x_ref, g_pre_ref, w_in_ref, ln_g_ref, ln_b_ref, wsp_ref, bsp_ref, g_a_ref,
                    va_ref, an_ref, q_ref, kb_ref, vb_ref, k_ref, v_ref, outa_ref, *, q_scale):
    tm = x_ref.shape[0]
    h = _rms(x_ref[...], g_pre_ref[...]).astype(jnp.bfloat16)

    def proj(col):
        return _dot(h, w_in_ref[:, col:col + SGU_WIDTH])

    u = _gelu(proj(0))
    gv = _gelu(proj(SGU_WIDTH))
    mu = jnp.mean(gv, axis=-1, keepdims=True)
    gc = gv - mu
    va = gc * lax.rsqrt(jnp.mean(gc * gc, axis=-1, keepdims=True) + EPS) * ln_g_ref[...] + ln_b_ref[...]
    va_ref[...] = va
    va_b = va.astype(jnp.bfloat16)
    for c in range(tm // SGU_CHUNK):
        r0 = c * SGU_CHUNK
        for hd in range(SGU_HEADS):
            c0 = hd * SGU_HEAD_DIM
            mixed = _dot(wsp_ref[hd], va_b[r0:r0 + SGU_CHUNK, c0:c0 + SGU_HEAD_DIM])
            mixed = mixed + bsp_ref[:, c0:c0 + SGU_HEAD_DIM]
            outa_ref[r0:r0 + SGU_CHUNK, c0:c0 + SGU_HEAD_DIM] = u[r0:r0 + SGU_CHUNK, c0:c0 + SGU_HEAD_DIM] * mixed
    an_ref[...] = _rms(outa_ref[...], g_a_ref[...]).astype(jnp.bfloat16)

    o = 2 * SGU_WIDTH
    q_ref[...] = (proj(o) * q_scale).astype(jnp.bfloat16)
    k = proj(o + SB_WIDTH)
    k_ref[...] = k
    kb_ref[...] = k.astype(jnp.bfloat16)
    v = proj(o + 2 * SB_WIDTH)
    v_ref[...] = v
    vb_ref[...] = v.astype(jnp.bfloat16)


def _in_proj(x, g_pre, w_in_b, ln_g, ln_b, wsp, bsp, g_a, *, tm):
    m = x.shape[0]
    in_width = w_in_b.shape[1]
    row = lambda i: (i, 0)
    fixed2 = lambda i: (0, 0)
    fixed3 = lambda i: (0, 0, 0)
    f32 = jnp.float32
    bf16 = jnp.bfloat16
    wide = lambda dt: jax.ShapeDtypeStruct((m, SGU_WIDTH), dt)
    blk = pl.BlockSpec((tm, SGU_WIDTH), row)
    return pl.pallas_call(
        functools.partial(_in_proj_kernel, q_scale=SB_HEAD_DIM ** -0.5),
        grid=(m // tm,),
        in_specs=[
            pl.BlockSpec((tm, D_MODEL), row),
            pl.BlockSpec((1, D_MODEL), fixed2),
            pl.BlockSpec((D_MODEL, in_width), fixed2),
            pl.BlockSpec((1, SGU_WIDTH), fixed2),
            pl.BlockSpec((1, SGU_WIDTH), fixed2),
            pl.BlockSpec((SGU_HEADS, SGU_CHUNK, SGU_CHUNK), fixed3),
            pl.BlockSpec((SGU_CHUNK, SGU_WIDTH), fixed2),
            pl.BlockSpec((1, SGU_WIDTH), fixed2),
        ],
        out_specs=[blk] * 7,
        out_shape=[wide(f32), wide(bf16), wide(bf16), wide(bf16), wide(bf16), wide(f32), wide(f32)],
        scratch_shapes=[pltpu.VMEM((tm, SGU_WIDTH), f32)],
        compiler_params=pltpu.CompilerParams(
            dimension_semantics=("arbitrary",), vmem_limit_bytes=V7X_VMEM_LIMIT_BYTES),
        name="in_proj",
    )(x, g_pre, w_in_b, ln_g, ln_b, wsp, bsp, g_a)


def _sb_attn_kernel(q_ref, k_ref, v_ref, u_ref, o_ref, acc_ref, carry_ref, *, past, tq):
    i = pl.program_id(2)
    lo = past + i * tq
    hi = lo + tq - 1
    n_full = lo // KEY_TILE
    n_tot = jnp.maximum(hi - 1, 0) // KEY_TILE + 1

    lane = lax.broadcasted_iota(jnp.int32, (tq, V7X_LANES), 1)
    q2 = q_ref[0]
    zero = jnp.zeros_like(q2)
    q_heads = [jnp.where((lane // SB_HEAD_DIM) == hd, q2, zero) for hd in range(HEADS_PER_BLOCK)]
    u_neg = u_ref[...]

    acc_ref[...] = jnp.zeros_like(acc_ref)
    carry_ref[...] = jnp.zeros_like(carry_ref)

    def tile(j, masked):
        for sb in reversed(range(KEY_SUBS_PER_TILE)):
            k0 = pl.multiple_of(j * KEY_TILE + sb * KEY_SUB, KEY_SUB)
            k2 = k_ref[0, pl.ds(k0, KEY_SUB), :]
            v2 = v_ref[0, pl.ds(k0, KEY_SUB), :]
            if masked:
                q_pos = lo + lax.broadcasted_iota(jnp.int32, (tq, KEY_SUB), 0)
                k_pos = k0 + lax.broadcasted_iota(jnp.int32, (tq, KEY_SUB), 1)
                reach = k_pos < q_pos
            for hd in range(HEADS_PER_BLOCK):
                z = lax.dot_general(q_heads[hd], k2, (((1,), (1,)), ((), ())),
                                    preferred_element_type=jnp.float32)
                if masked:
                    z = jnp.where(reach, z, UNREACHABLE)
                zb = z.astype(jnp.bfloat16)
                l = jnp.log(1.0 + jnp.exp(-jnp.abs(zb)))
                sp = jnp.maximum(zb, 0.0) + l
                own = (jnp.minimum(zb, 0.0) - l).astype(jnp.float32)
                excl = _dot(sp, u_neg)
                w = jnp.exp(own + excl)
                carry = carry_ref[hd]
                acc_ref[hd] += jnp.exp(carry) * _dot(w.astype(jnp.bfloat16), v2)
                carry_ref[hd] = carry + excl[:, 0:1] - sp[:, 0:1].astype(jnp.float32)

    def masked_body(t, c):
        tile(n_tot - 1 - t, True)
        return c

    def full_body(t, c):
        tile(n_full - 1 - t, False)
        return c

    lax.fori_loop(0, n_tot - n_full, masked_body, 0)
    lax.fori_loop(0, n_full, full_body, 0)

    out = acc_ref[0]
    for hd in range(1, HEADS_PER_BLOCK):
        out = jnp.where((lane // SB_HEAD_DIM) == hd, acc_ref[hd], out)
    o_ref[0] = out


def _sb_attn(q, k_all, v_all, u_neg, *, past, tq):
    b, t, _ = q.shape
    tk = k_all.shape[1]
    n_blk = SB_WIDTH // V7X_LANES
    return pl.pallas_call(
        functools.partial(_sb_attn_kernel, past=past, tq=tq),
        grid=(b, n_blk, t // tq),
        in_specs=[
            pl.BlockSpec((1, tq, V7X_LANES), lambda bi, hp, i: (bi, i, hp)),
            pl.BlockSpec((1, tk, V7X_LANES), lambda bi, hp, i: (bi, 0, hp)),
            pl.BlockSpec((1, tk, V7X_LANES), lambda bi, hp, i: (bi, 0, hp)),
            pl.BlockSpec((KEY_SUB, KEY_SUB), lambda bi, hp, i: (0, 0)),
        ],
        out_specs=pl.BlockSpec((1, tq, V7X_LANES), lambda bi, hp, i: (bi, i, hp)),
        out_shape=jax.ShapeDtypeStruct((b, t, SB_WIDTH), jnp.float32),
        scratch_shapes=[
            pltpu.VMEM((HEADS_PER_BLOCK, tq, V7X_LANES), jnp.float32),
            pltpu.VMEM((HEADS_PER_BLOCK, tq, 1), jnp.float32),
        ],
        compiler_params=pltpu.CompilerParams(
            dimension_semantics=("arbitrary", "arbitrary", "arbitrary"),
            vmem_limit_bytes=V7X_VMEM_LIMIT_BYTES),
        name="sb_attn",
    )(q, k_all, v_all, u_neg)


def _sb_attn_short_kernel(qbd_ref, k_ref, v_ref, ut_ref, o_ref, *, past, t):
    n_sub = k_ref.shape[1] // KEY_SUB
    lane = lax.broadcasted_iota(jnp.int32, (KEY_SUB, V7X_LANES), 1)
    row = lax.broadcasted_iota(jnp.int32, (KEY_SUB, V7X_LANES), 0)
    q_pos = past + lane % t
    qbd = qbd_ref[0]
    ut = ut_ref[...]
    rows = [slice(j * KEY_SUB, (j + 1) * KEY_SUB) for j in range(n_sub)]
    zs = [_dot(k_ref[0, rows[j], :], qbd) for j in range(n_sub)]
    expo, tot = [], []
    for j in range(n_sub):
        z = zs[j]
        if (j + 1) * KEY_SUB > past:
            z = jnp.where(row + j * KEY_SUB < q_pos, z, UNREACHABLE)
        zb = z.astype(jnp.bfloat16)
        l = jnp.log(1.0 + jnp.exp(-jnp.abs(zb)))
        sp = jnp.maximum(zb, 0.0) + l
        own = (jnp.minimum(zb, 0.0) - l).astype(jnp.float32)
        excl = _dot(ut, sp)
        expo.append(own + excl)
        tot.append(excl[0:1, :] - sp[0:1, :].astype(jnp.float32))
    carry = jnp.zeros((1, V7X_LANES), jnp.float32)
    acc = jnp.zeros((V7X_LANES, SB_WIDTH), jnp.float32)
    for j in reversed(range(n_sub)):
        w = jnp.exp(expo[j] + carry)
        acc = acc + _dot(w.T.astype(jnp.bfloat16), v_ref[0, rows[j], :])
        carry = carry + tot[j]
    col_head = lax.broadcasted_iota(jnp.int32, (t, SB_WIDTH), 1) // SB_HEAD_DIM
    out = jnp.zeros((t, SB_WIDTH), jnp.float32)
    for hd in range(SB_HEADS):
        out = jnp.where(col_head == hd, acc[hd * t:(hd + 1) * t, :], out)
    o_ref[0] = out


def _sb_attn_short(q, k_all, v_all, u_neg, *, past):
    b, t, _ = q.shape
    assert SB_HEADS * t == V7X_LANES
    tk = k_all.shape[1]
    q4 = q.reshape(b, t, SB_HEADS, SB_HEAD_DIM)
    qbd = jnp.einsum("bthd,hg->bhdgt", q4, jnp.eye(SB_HEADS, dtype=q.dtype)).reshape(b, SB_WIDTH, V7X_LANES)
    return pl.pallas_call(
        functools.partial(_sb_attn_short_kernel, past=past, t=t),
        grid=(b,),
        in_specs=[
            pl.BlockSpec((1, SB_WIDTH, V7X_LANES), lambda bi: (bi, 0, 0)),
            pl.BlockSpec((1, tk, SB_WIDTH), lambda bi: (bi, 0, 0)),
            pl.BlockSpec((1, tk, SB_WIDTH), lambda bi: (bi, 0, 0)),
            pl.BlockSpec((KEY_SUB, KEY_SUB), lambda bi: (0, 0)),
        ],
        out_specs=pl.BlockSpec((1, t, SB_WIDTH), lambda bi: (bi, 0, 0)),
        out_shape=jax.ShapeDtypeStruct((b, t, SB_WIDTH), jnp.float32),
        compiler_params=pltpu.CompilerParams(
            dimension_semantics=("arbitrary",), vmem_limit_bytes=V7X_VMEM_LIMIT_BYTES),
        name="sb_attn_short",
    )(qbd, k_all, v_all, u_neg.T)


def _mix_ffn_kernel(*refs, seq_tiles, has_hist):
    if has_hist:
        (x_ref, an_ref, ob_ref, h1_ref, h2_ref, g_b_ref, w_out_ref, g_pm_ref, g_pf_ref, w_up_ref,
         cw_ref, cb_ref, w_dn_ref, g_po_ref, y_ref, up_ref, buf_ref) = refs
    else:
        (x_ref, an_ref, ob_ref, g_b_ref, w_out_ref, g_pm_ref, g_pf_ref, w_up_ref,
         cw_ref, cb_ref, w_dn_ref, g_po_ref, y_ref, hist_ref, buf_ref) = refs
    tm = x_ref.shape[0]
    pad = V7X_SUBLANES

    bn = _rms(ob_ref[...], g_b_ref[...]).astype(jnp.bfloat16)
    mix = _dot(an_ref[...], w_out_ref[0:SGU_WIDTH, :]) + _dot(bn, w_out_ref[SGU_WIDTH:, :])
    x1 = x_ref[...] + _rms(mix, g_pm_ref[...])
    h2 = _rms(x1, g_pf_ref[...]).astype(jnp.bfloat16)

    if has_hist:
        t_in_seq = lax.broadcasted_iota(jnp.int32, (tm, FF_CHUNK), 0) % seq_tiles
        buf_ref[0:pad, :] = jnp.zeros((pad, buf_ref.shape[1]), jnp.float32)
    else:
        @pl.when(pl.program_id(0) % seq_tiles == 0)
        def _():
            buf_ref[0:pad, :] = jnp.zeros((pad, buf_ref.shape[1]), jnp.float32)

    def up_dot(col):
        buf_ref[pad:pad + tm, col:col + FF_CHUNK] = _dot(h2, w_up_ref[:, col:col + FF_CHUNK])

    def conv(col, scale):
        up = buf_ref[pad:pad + tm, col:col + FF_CHUNK]
        m1 = buf_ref[pad - 1:pad - 1 + tm, col:col + FF_CHUNK]
        m2 = buf_ref[pad - 2:pad - 2 + tm, col:col + FF_CHUNK]
        if has_hist:
            up_ref[:, col:col + FF_CHUNK] = up
            m1 = jnp.where(t_in_seq >= 1, m1, 0.0) + h1_ref[:, col:col + FF_CHUNK]
            m2 = jnp.where(t_in_seq >= 2, m2, 0.0) + h2_ref[:, col:col + FF_CHUNK]
        else:
            hist_ref[0, :, col:col + FF_CHUNK] = up[tm - 2:tm, :]
            buf_ref[pad - 2:pad, col:col + FF_CHUNK] = up[tm - 2:tm, :]
        w = cw_ref[:, col:col + FF_CHUNK] * scale
        b = cb_ref[:, col:col + FF_CHUNK] * scale
        return m2 * w[0:1] + m1 * w[1:2] + up * w[2:3] + b

    n_chunks = D_FF // FF_CHUNK
    up_dot(0)
    up_dot(D_FF)
    f = jnp.zeros((tm, D_MODEL), jnp.float32)
    prev_act = None
    for c in range(n_chunks):
        col = c * FF_CHUNK
        if c + 1 < n_chunks:
            up_dot(col + FF_CHUNK)
            up_dot(D_FF + col + FF_CHUNK)
        if prev_act is not None:
            f = f + _dot(prev_act, w_dn_ref[col - FF_CHUNK:col, :])
        gate = conv(col, 1.0)
        half_val = conv(D_FF + col, 0.5)
        inner = gate * (GELU_C1 + GELU_C2 * (gate * gate))
        prev_act = ((gate * half_val) * (1.0 + jnp.tanh(inner))).astype(jnp.bfloat16)
    f = f + _dot(prev_act, w_dn_ref[D_FF - FF_CHUNK:D_FF, :])
    y_ref[...] = x1 + _rms(f, g_po_ref[...])


def _mix_ffn(x, an, ob, hist_taps, g_b, w_out_b, g_pm, g_pf, w_up_b, conv_w, conv_b, w_dn_b, g_po,
             *, tm, seq_len):
    m = x.shape[0]
    row = lambda i: (i, 0)
    fixed = lambda i: (0, 0)
    has_hist = hist_taps is not None
    once = dict(pipeline_mode=pl.Buffered(1))
    in_specs = [
        pl.BlockSpec((tm, D_MODEL), row),
        pl.BlockSpec((tm, SGU_WIDTH), row),
        pl.BlockSpec((tm, SB_WIDTH), row),
    ]
    args = [x, an, ob]
    if has_hist:
        assert tm % seq_len == 0
        seq_tiles = seq_len
        in_specs += [pl.BlockSpec((tm, 2 * D_FF), row)] * 2
        args += list(hist_taps)
        out_specs = [pl.BlockSpec((tm, D_MODEL), row), pl.BlockSpec((tm, 2 * D_FF), row)]
        out_shape = [jax.ShapeDtypeStruct((m, D_MODEL), jnp.float32),
                     jax.ShapeDtypeStruct((m, 2 * D_FF), jnp.float32)]
    else:
        assert seq_len % tm == 0
        seq_tiles = seq_len // tm
        out_specs = [pl.BlockSpec((tm, D_MODEL), row),
                     pl.BlockSpec((1, CONV_WIDTH - 1, 2 * D_FF), lambda i: (i // seq_tiles, 0, 0))]
        out_shape = [jax.ShapeDtypeStruct((m, D_MODEL), jnp.float32),
                     jax.ShapeDtypeStruct((m // seq_len, CONV_WIDTH - 1, 2 * D_FF), jnp.float32)]
    in_specs += [
        pl.BlockSpec((1, SB_WIDTH), fixed),
        pl.BlockSpec((D_MODEL, D_MODEL), fixed, **once),
        pl.BlockSpec((1, D_MODEL), fixed),
        pl.BlockSpec((1, D_MODEL), fixed),
        pl.BlockSpec((D_MODEL, 2 * D_FF), fixed, **once),
        pl.BlockSpec((CONV_WIDTH, 2 * D_FF), fixed),
        pl.BlockSpec((1, 2 * D_FF), fixed),
        pl.BlockSpec((D_FF, D_MODEL), fixed, **once),
        pl.BlockSpec((1, D_MODEL), fixed),
    ]
    args += [g_b, w_out_b, g_pm, g_pf, w_up_b, conv_w, conv_b, w_dn_b, g_po]
    return pl.pallas_call(
        functools.partial(_mix_ffn_kernel, seq_tiles=seq_tiles, has_hist=has_hist),
        grid=(m // tm,),
        in_specs=in_specs,
        out_specs=out_specs,
        out_shape=out_shape,
        scratch_shapes=[pltpu.VMEM((tm + V7X_SUBLANES, 2 * D_FF), jnp.float32)],
        compiler_params=pltpu.CompilerParams(
            dimension_semantics=("arbitrary",), vmem_limit_bytes=V7X_VMEM_LIMIT_BYTES),
        name="mix_ffn",
    )(*args)


def _suffix_matrix():
    r = lax.broadcasted_iota(jnp.int32, (KEY_SUB, KEY_SUB), 0)
    c = lax.broadcasted_iota(jnp.int32, (KEY_SUB, KEY_SUB), 1)
    return jnp.where(r > c, -1.0, 0.0).astype(jnp.bfloat16)


def _layer(x, past_k, past_v, conv_hist, p, *, tm_proj, tq, tm_ffn):
    bsz, t, _ = x.shape
    m = bsz * t
    row = lambda a: a.reshape(1, -1)
    bf16 = jnp.bfloat16

    seq = min(t, SGU_CHUNK)
    w_s = jnp.where(jnp.tril(jnp.ones((seq, seq), bool))[None], p["w_spatial"][:, :seq, :seq], 0.0)
    reps = SGU_CHUNK // seq
    wsp = jnp.einsum("ab,hts->hatbs", jnp.eye(reps, dtype=w_s.dtype), w_s)
    wsp = wsp.reshape(SGU_HEADS, SGU_CHUNK, SGU_CHUNK).astype(bf16)
    b_s = jnp.tile(p["b_spatial"][:, :seq].T, (reps, 1))
    bsp = jnp.repeat(b_s, SGU_HEAD_DIM, axis=1)

    va, an, q, kb, vb, k, v = _in_proj(
        x.reshape(m, D_MODEL), row(p["g_pre_mix"]), p["w_in"].astype(bf16), row(p["ln_v_g"]),
        row(p["ln_v_b"]), wsp, bsp, row(p["g_out_a"]), tm=tm_proj)

    past = past_k.shape[1]
    short = SB_HEADS * t == V7X_LANES
    key_pad = KEY_SUB if short else KEY_TILE
    t_keys = -(-(past + t) // key_pad) * key_pad
    fill = jnp.zeros((bsz, t_keys - past - t, SB_WIDTH), bf16)
    cat = lambda old, new: jnp.concatenate(
        [old.reshape(bsz, past, SB_WIDTH).astype(bf16), new.reshape(bsz, t, SB_WIDTH), fill], axis=1)
    q3 = q.reshape(bsz, t, SB_WIDTH)
    if short:
        ob = _sb_attn_short(q3, cat(past_k, kb), cat(past_v, vb), _suffix_matrix(), past=past)
    else:
        ob = _sb_attn(q3, cat(past_k, kb), cat(past_v, vb), _suffix_matrix(), past=past, tq=tq)

    if conv_hist is None:
        hist_taps = None
    else:
        z = jnp.zeros((bsz, t - 1, 2 * D_FF), jnp.float32)
        tap1 = jnp.concatenate([conv_hist[:, 1:], z], axis=1)
        tap2 = jnp.concatenate([conv_hist, z[:, 1:]], axis=1)
        hist_taps = (tap1.reshape(m, 2 * D_FF), tap2.reshape(m, 2 * D_FF))
    y, extra = _mix_ffn(
        x.reshape(m, D_MODEL), an, ob.reshape(m, SB_WIDTH), hist_taps, row(p["g_out_b"]),
        p["w_out"].astype(bf16), row(p["g_post_mix"]), row(p["g_pre_ffn"]), p["w_up"].astype(bf16),
        p["conv_w"], row(p["conv_b"]), p["w_down"].astype(bf16), row(p["g_post_ffn"]),
        tm=tm_ffn, seq_len=t)
    if conv_hist is None:
        new_hist = extra
    else:
        new_hist = extra.reshape(bsz, t, 2 * D_FF)[:, t - (CONV_WIDTH - 1):]
    shape_kv = (bsz, t, SB_HEADS, SB_HEAD_DIM)
    return (y.reshape(bsz, t, D_MODEL), k.reshape(shape_kv), v.reshape(shape_kv),
            va.reshape(bsz, t, SGU_WIDTH), new_hist)


def kernel(x_prompt, x_sample, cache_sb_k, cache_sb_v, cache_ffn_conv, w_in, g_pre_mix, ln_v_g, ln_v_b, w_spatial, b_spatial, g_out_a, g_out_b, w_out, g_post_mix, g_pre_ffn, w_up, conv_w, conv_b, w_down, g_post_ffn):
    names = ("w_in", "g_pre_mix", "ln_v_g", "ln_v_b", "w_spatial", "b_spatial", "g_out_a", "g_out_b",
             "w_out", "g_post_mix", "g_pre_ffn", "w_up", "conv_w", "conv_b", "w_down", "g_post_ffn")
    stacked = (w_in, g_pre_mix, ln_v_g, ln_v_b, w_spatial, b_spatial, g_out_a, g_out_b, w_out,
               g_post_mix, g_pre_ffn, w_up, conv_w, conv_b, w_down, g_post_ffn)
    depth = w_in.shape[0]
    yp, ys = x_prompt, x_sample
    bp = x_prompt.shape[0]
    empty_kv = jnp.zeros((bp, 0, SB_HEADS, SB_HEAD_DIM), x_prompt.dtype)
    outs = [[] for _ in range(7)]
    for l in range(depth):
        p = {n: a[l] for n, a in zip(names, stacked)}
        yp, kp, vp, _, cp = _layer(yp, empty_kv, empty_kv, None, p, tm_proj=512, tq=1024, tm_ffn=256)
        ys, ksm, vsm, vas, cs = _layer(ys, cache_sb_k[l], cache_sb_v[l], cache_ffn_conv[l], p,
                                       tm_proj=512, tq=ys.shape[1], tm_ffn=128)
        for lst, a in zip(outs, (kp, vp, ksm, vsm, vas, cp, cs)):
            lst.append(a)
    return (yp, ys) + tuple(jnp.stack(lst) for lst in outs)
```

```python
import functools
import math

import jax
import jax.numpy as jnp
from jax import lax
from jax.experimental import pallas as pl
from jax.experimental.pallas import tpu as pltpu

D_MODEL = 1024
SGU_WIDTH = 512
SGU_HEADS = 4
SGU_HEAD_DIM = 128
SGU_CHUNK = 128
SB_WIDTH = 512
SB_HEADS = 8
SB_HEAD_DIM = 64
D_FF = 2816
CONV_WIDTH = 3
EPS = 1e-6

V7X_LANES = 128
V7X_SUBLANES = 8
V7X_MXU_DIM = 256
V7X_VMEM_LIMIT_BYTES = 56 * 1024 * 1024

HEADS_PER_BLOCK = V7X_LANES // SB_HEAD_DIM
KEY_SUB = V7X_MXU_DIM
KEY_SUBS_PER_TILE = 4
KEY_TILE = KEY_SUB * KEY_SUBS_PER_TILE
FF_CHUNK = V7X_MXU_DIM
UNREACHABLE = -1e30
GELU_C1 = math.sqrt(2.0 / math.pi)
GELU_C2 = 0.044715 * GELU_C1


def _rms(x, g):
    return x * lax.rsqrt(jnp.mean(x * x, axis=-1, keepdims=True) + EPS) * g


def _gelu(x):
    return jax.nn.gelu(x)


def _dot(a, b):
    return jnp.dot(a, b, preferred_element_type=jnp.float32)


def _in_proj_kernel(x_ref, g_pre_ref, w_in_ref, ln_g_ref, ln_b_ref, wsp_ref, bsp_ref, g_a_ref,
                    va_ref, an_ref, q_ref, kb_ref, vb_ref, k_ref, v_ref, outa_ref, *, q_scale):
    tm = x_ref.shape[0]
    h = _rms(x_ref[...], g_pre_ref[...]).astype(jnp.bfloat16)

    def proj(col):
        return _dot(h, w_in_ref[:, col:col + SGU_WIDTH])

    u = _gelu(proj(0))
    gv = _gelu(proj(SGU_WIDTH))
    mu = jnp.mean(gv, axis=-1, keepdims=True)
    gc = gv - mu
    va = gc * lax.rsqrt(jnp.mean(gc * gc, axis=-1, keepdims=True) + EPS) * ln_g_ref[...] + ln_b_ref[...]
    va_ref[...] = va
    va_b = va.astype(jnp.bfloat16)
    for c in range(tm // SGU_CHUNK):
        r0 = c * SGU_CHUNK
        for hd in range(SGU_HEADS):
            c0 = hd * SGU_HEAD_DIM
            mixed = _dot(wsp_ref[hd], va_b[r0:r0 + SGU_CHUNK, c0:c0 + SGU_HEAD_DIM])
            mixed = mixed + bsp_ref[:, c0:c0 + SGU_HEAD_DIM]
            outa_ref[r0:r0 + SGU_CHUNK, c0:c0 + SGU_HEAD_DIM] = u[r0:r0 + SGU_CHUNK, c0:c0 + SGU_HEAD_DIM] * mixed
    an_ref[...] = _rms(outa_ref[...], g_a_ref[...]).astype(jnp.bfloat16)

    o = 2 * SGU_WIDTH
    q_ref[...] = (proj(o) * q_scale).astype(jnp.bfloat16)
    k = proj(o + SB_WIDTH)
    k_ref[...] = k
    kb_ref[...] = k.astype(jnp.bfloat16)
    v = proj(o + 2 * SB_WIDTH)
    v_ref[...] = v
    vb_ref[...] = v.astype(jnp.bfloat16)


def _in_proj(x, g_pre, w_in_b, ln_g, ln_b, wsp, bsp, g_a, *, tm):
    m = x.shape[0]
    in_width = w_in_b.shape[1]
    row = lambda i: (i, 0)
    fixed2 = lambda i: (0, 0)
    fixed3 = lambda i: (0, 0, 0)
    f32 = jnp.float32
    bf16 = jnp.bfloat16
    wide = lambda dt: jax.ShapeDtypeStruct((m, SGU_WIDTH), dt)
    blk = pl.BlockSpec((tm, SGU_WIDTH), row)
    return pl.pallas_call(
        functools.partial(_in_proj_kernel, q_scale=SB_HEAD_DIM ** -0.5),
        grid=(m // tm,),
        in_specs=[
            pl.BlockSpec((tm, D_MODEL), row),
            pl.BlockSpec((1, D_MODEL), fixed2),
            pl.BlockSpec((D_MODEL, in_width), fixed2),
            pl.BlockSpec((1, SGU_WIDTH), fixed2),
            pl.BlockSpec((1, SGU_WIDTH), fixed2),
            pl.BlockSpec((SGU_HEADS, SGU_CHUNK, SGU_CHUNK), fixed3),
            pl.BlockSpec((SGU_CHUNK, SGU_WIDTH), fixed2),
            pl.BlockSpec((1, SGU_WIDTH), fixed2),
        ],
        out_specs=[blk] * 7,
        out_shape=[wide(f32), wide(bf16), wide(bf16), wide(bf16), wide(bf16), wide(f32), wide(f32)],
        scratch_shapes=[pltpu.VMEM((tm, SGU_WIDTH), f32)],
        compiler_params=pltpu.CompilerParams(
            dimension_semantics=("arbitrary",), vmem_limit_bytes=V7X_VMEM_LIMIT_BYTES),
        name="in_proj",
    )(x, g_pre, w_in_b, ln_g, ln_b, wsp, bsp, g_a)


def _sb_attn_kernel(q_ref, k_ref, v_ref, u_ref, o_ref, acc_ref, carry_ref, *, past, tq):
    i = pl.program_id(2)
    lo = past + i * tq
    hi = lo + tq - 1
    n_full = lo // KEY_TILE
    n_tot = jnp.maximum(hi - 1, 0) // KEY_TILE + 1

    lane = lax.broadcasted_iota(jnp.int32, (tq, V7X_LANES), 1)
    q2 = q_ref[0]
    zero = jnp.zeros_like(q2)
    q_heads = [jnp.where((lane // SB_HEAD_DIM) == hd, q2, zero) for hd in range(HEADS_PER_BLOCK)]
    u_neg = u_ref[...]

    acc_ref[...] = jnp.zeros_like(acc_ref)
    carry_ref[...] = jnp.zeros_like(carry_ref)

    diagonal = tq == KEY_TILE and past % KEY_TILE == 0

    def tile(j, masked):
        for sb in reversed(range(KEY_SUBS_PER_TILE)):
            k0 = pl.multiple_of(j * KEY_TILE + sb * KEY_SUB, KEY_SUB)
            k2 = k_ref[0, pl.ds(k0, KEY_SUB), :]
            v2 = v_ref[0, pl.ds(k0, KEY_SUB), :]
            r0 = sb * KEY_SUB if (masked and diagonal) else 0
            rows = slice(r0, tq)
            if masked:
                q_pos = lo + r0 + lax.broadcasted_iota(jnp.int32, (tq - r0, KEY_SUB), 0)
                k_pos = k0 + lax.broadcasted_iota(jnp.int32, (tq - r0, KEY_SUB), 1)
                reach = k_pos < q_pos
            for hd in range(HEADS_PER_BLOCK):
                z = lax.dot_general(q_heads[hd][rows], k2, (((1,), (1,)), ((), ())),
                                    preferred_element_type=jnp.float32)
                if masked:
                    z = jnp.where(reach, z, UNREACHABLE)
                zb = z.astype(jnp.bfloat16)
                z_pos = jnp.maximum(zb, 0.0)
                z_neg = jnp.minimum(zb, 0.0)
                l = jnp.log(1.0 + jnp.exp(z_neg - z_pos))
                sp = z_pos + l
                own = (z_neg - l).astype(jnp.float32)
                excl = _dot(sp, u_neg)
                carry = carry_ref[hd, rows]
                w = jnp.exp((own + excl + carry).astype(jnp.bfloat16))
                acc_ref[hd, rows] += _dot(w, v2)
                carry_ref[hd, rows] = carry + excl[:, 0:1] - sp[:, 0:1].astype(jnp.float32)

    def masked_body(t, c):
        tile(n_tot - 1 - t, True)
        return c

    def full_body(t, c):
        tile(n_full - 1 - t, False)
        return c

    lax.fori_loop(0, n_tot - n_full, masked_body, 0)
    lax.fori_loop(0, n_full, full_body, 0)

    out = acc_ref[0]
    for hd in range(1, HEADS_PER_BLOCK):
        out = jnp.where((lane // SB_HEAD_DIM) == hd, acc_ref[hd], out)
    o_ref[0] = out


def _sb_attn(q, k_all, v_all, u_neg, *, past, tq):
    b, t, _ = q.shape
    tk = k_all.shape[1]
    n_blk = SB_WIDTH // V7X_LANES
    return pl.pallas_call(
        functools.partial(_sb_attn_kernel, past=past, tq=tq),
        grid=(b, n_blk, t // tq),
        in_specs=[
            pl.BlockSpec((1, tq, V7X_LANES), lambda bi, hp, i: (bi, i, hp)),
            pl.BlockSpec((1, tk, V7X_LANES), lambda bi, hp, i: (bi, 0, hp)),
            pl.BlockSpec((1, tk, V7X_LANES), lambda bi, hp, i: (bi, 0, hp)),
            pl.BlockSpec((KEY_SUB, KEY_SUB), lambda bi, hp, i: (0, 0)),
        ],
        out_specs=pl.BlockSpec((1, tq, V7X_LANES), lambda bi, hp, i: (bi, i, hp)),
        out_shape=jax.ShapeDtypeStruct((b, t, SB_WIDTH), jnp.float32),
        scratch_shapes=[
            pltpu.VMEM((HEADS_PER_BLOCK, tq, V7X_LANES), jnp.float32),
            pltpu.VMEM((HEADS_PER_BLOCK, tq, 1), jnp.float32),
        ],
        compiler_params=pltpu.CompilerParams(
            dimension_semantics=("arbitrary", "arbitrary", "arbitrary"),
            vmem_limit_bytes=V7X_VMEM_LIMIT_BYTES),
        name="sb_attn",
    )(q, k_all, v_all, u_neg)


def _sb_attn_short_kernel(qbd_ref, k_ref, v_ref, ut_ref, o_ref, *, past, t):
    n_sub = k_ref.shape[1] // KEY_SUB
    lane = lax.broadcasted_iota(jnp.int32, (KEY_SUB, V7X_LANES), 1)
    row = lax.broadcasted_iota(jnp.int32, (KEY_SUB, V7X_LANES), 0)
    q_pos = past + lane % t
    qbd = qbd_ref[0]
    ut = ut_ref[...]
    rows = [slice(j * KEY_SUB, (j + 1) * KEY_SUB) for j in range(n_sub)]
    zs = [_dot(k_ref[0, rows[j], :], qbd) for j in range(n_sub)]
    expo, tot = [], []
    for j in range(n_sub):
        z = zs[j]
        if (j + 1) * KEY_SUB > past:
            z = jnp.where(row + j * KEY_SUB < q_pos, z, UNREACHABLE)
        zb = z.astype(jnp.bfloat16)
        l = jnp.log(1.0 + jnp.exp(-jnp.abs(zb)))
        sp = jnp.maximum(zb, 0.0) + l
        own = (jnp.minimum(zb, 0.0) - l).astype(jnp.float32)
        excl = _dot(ut, sp)
        expo.append(own + excl)
        tot.append(excl[0:1, :] - sp[0:1, :].astype(jnp.float32))
    carry = jnp.zeros((1, V7X_LANES), jnp.float32)
    acc = jnp.zeros((V7X_LANES, SB_WIDTH), jnp.float32)
    for j in reversed(range(n_sub)):
        w = jnp.exp(expo[j] + carry)
        acc = acc + _dot(w.T.astype(jnp.bfloat16), v_ref[0, rows[j], :])
        carry = carry + tot[j]
    col_head = lax.broadcasted_iota(jnp.int32, (t, SB_WIDTH), 1) // SB_HEAD_DIM
    out = jnp.zeros((t, SB_WIDTH), jnp.float32)
    for hd in range(SB_HEADS):
        out = jnp.where(col_head == hd, acc[hd * t:(hd + 1) * t, :], out)
    o_ref[0] = out


def _sb_attn_short(q, k_all, v_all, u_neg, *, past):
    b, t, _ = q.shape
    assert SB_HEADS * t == V7X_LANES
    tk = k_all.shape[1]
    q4 = q.reshape(b, t, SB_HEADS, SB_HEAD_DIM)
    qbd = jnp.einsum("bthd,hg->bhdgt", q4, jnp.eye(SB_HEADS, dtype=q.dtype)).reshape(b, SB_WIDTH, V7X_LANES)
    return pl.pallas_call(
        functools.partial(_sb_attn_short_kernel, past=past, t=t),
        grid=(b,),
        in_specs=[
            pl.BlockSpec((1, SB_WIDTH, V7X_LANES), lambda bi: (bi, 0, 0)),
            pl.BlockSpec((1, tk, SB_WIDTH), lambda bi: (bi, 0, 0)),
            pl.BlockSpec((1, tk, SB_WIDTH), lambda bi: (bi, 0, 0)),
            pl.BlockSpec((KEY_SUB, KEY_SUB), lambda bi: (0, 0)),
        ],
        out_specs=pl.BlockSpec((1, t, SB_WIDTH), lambda bi: (bi, 0, 0)),
        out_shape=jax.ShapeDtypeStruct((b, t, SB_WIDTH), jnp.float32),
        compiler_params=pltpu.CompilerParams(
            dimension_semantics=("arbitrary",), vmem_limit_bytes=V7X_VMEM_LIMIT_BYTES),
        name="sb_attn_short",
    )(qbd, k_all, v_all, u_neg.T)


def _mix_ffn_kernel(*refs, seq_tiles, has_hist):
    if has_hist:
        (x_ref, an_ref, ob_ref, h1_ref, h2_ref, g_b_ref, w_out_ref, g_pm_ref, g_pf_ref, w_up_ref,
         cw_ref, cb_ref, w_dn_ref, g_po_ref, y_ref, up_ref, buf_ref, x1_buf, hn_buf) = refs
    else:
        (x_ref, an_ref, ob_ref, g_b_ref, w_out_ref, g_pm_ref, g_pf_ref, w_up_ref,
         cw_ref, cb_ref, w_dn_ref, g_po_ref, y_ref, hist_ref, buf_ref, x1_buf, hn_buf) = refs
    tm = x_ref.shape[0]
    pad = V7X_SUBLANES
    step = pl.program_id(0)
    slot = step % 2
    prev = 1 - slot

    @pl.when(step == 0)
    def _():
        x1_buf[1] = jnp.zeros(x1_buf.shape[1:], x1_buf.dtype)
        hn_buf[1] = jnp.zeros(hn_buf.shape[1:], hn_buf.dtype)

    x1 = x1_buf[prev]
    h2 = hn_buf[prev]

    def mixer_epilogue():
        bn = _rms(ob_ref[...], g_b_ref[...]).astype(jnp.bfloat16)
        mix = _dot(an_ref[...], w_out_ref[0:SGU_WIDTH, :]) + _dot(bn, w_out_ref[SGU_WIDTH:, :])
        x1_new = x_ref[...] + _rms(mix, g_pm_ref[...])
        x1_buf[slot] = x1_new
        hn_buf[slot] = _rms(x1_new, g_pf_ref[...]).astype(jnp.bfloat16)

    if has_hist:
        t_in_seq = lax.broadcasted_iota(jnp.int32, (tm, FF_CHUNK), 0) % seq_tiles
        buf_ref[0:pad, :] = jnp.zeros((pad, buf_ref.shape[1]), jnp.float32)
    else:
        @pl.when((step - 1) % seq_tiles == 0)
        def _():
            buf_ref[0:pad, :] = jnp.zeros((pad, buf_ref.shape[1]), jnp.float32)

    def up_dot(col):
        buf_ref[pad:pad + tm, col:col + FF_CHUNK] = _dot(h2, w_up_ref[:, col:col + FF_CHUNK])

    def conv(col, scale):
        up = buf_ref[pad:pad + tm, col:col + FF_CHUNK]
        m1 = buf_ref[pad - 1:pad - 1 + tm, col:col + FF_CHUNK]
        m2 = buf_ref[pad - 2:pad - 2 + tm, col:col + FF_CHUNK]
        if has_hist:
            up_ref[:, col:col + FF_CHUNK] = up
            m1 = jnp.where(t_in_seq >= 1, m1, 0.0) + h1_ref[:, col:col + FF_CHUNK]
            m2 = jnp.where(t_in_seq >= 2, m2, 0.0) + h2_ref[:, col:col + FF_CHUNK]
        else:
            hist_ref[0, :, col:col + FF_CHUNK] = up[tm - 2:tm, :]
            buf_ref[pad - 2:pad, col:col + FF_CHUNK] = up[tm - 2:tm, :]
        w = cw_ref[:, col:col + FF_CHUNK] * scale
        b = cb_ref[:, col:col + FF_CHUNK] * scale
        return m2 * w[0:1] + m1 * w[1:2] + up * w[2:3] + b

    n_chunks = D_FF // FF_CHUNK
    up_dot(0)
    up_dot(D_FF)
    mixer_epilogue()
    f = jnp.zeros((tm, D_MODEL), jnp.float32)
    prev_act = None
    for c in range(n_chunks):
        col = c * FF_CHUNK
        if c + 1 < n_chunks:
            up_dot(col + FF_CHUNK)
            up_dot(D_FF + col + FF_CHUNK)
        if prev_act is not None:
            f = f + _dot(prev_act, w_dn_ref[col - FF_CHUNK:col, :])
        gate = conv(col, 1.0)
        half_val = conv(D_FF + col, 0.5)
        inner = gate * (GELU_C1 + GELU_C2 * (gate * gate))
        prev_act = ((gate * half_val) * (1.0 + jnp.tanh(inner))).astype(jnp.bfloat16)
    f = f + _dot(prev_act, w_dn_ref[D_FF - FF_CHUNK:D_FF, :])
    y_ref[...] = x1 + _rms(f, g_po_ref[...])


def _mix_ffn(x, an, ob, hist_taps, g_b, w_out_b, g_pm, g_pf, w_up_b, conv_w, conv_b, w_dn_b, g_po,
             *, tm, seq_len):
    m = x.shape[0]
    n_tiles = m // tm
    row = lambda i: (jnp.minimum(i, n_tiles - 1), 0)
    done = lambda i: (jnp.maximum(i - 1, 0), 0)
    fixed = lambda i: (0, 0)
    has_hist = hist_taps is not None
    once = dict(pipeline_mode=pl.Buffered(1))
    in_specs = [
        pl.BlockSpec((tm, D_MODEL), row),
        pl.BlockSpec((tm, SGU_WIDTH), row),
        pl.BlockSpec((tm, SB_WIDTH), row),
    ]
    args = [x, an, ob]
    if has_hist:
        assert tm % seq_len == 0
        seq_tiles = seq_len
        in_specs += [pl.BlockSpec((tm, 2 * D_FF), done)] * 2
        args += list(hist_taps)
        out_specs = [pl.BlockSpec((tm, D_MODEL), done), pl.BlockSpec((tm, 2 * D_FF), done)]
        out_shape = [jax.ShapeDtypeStruct((m, D_MODEL), jnp.float32),
                     jax.ShapeDtypeStruct((m, 2 * D_FF), jnp.float32)]
    else:
        assert seq_len % tm == 0
        seq_tiles = seq_len // tm
        out_specs = [pl.BlockSpec((tm, D_MODEL), done),
                     pl.BlockSpec((1, CONV_WIDTH - 1, 2 * D_FF),
                                  lambda i: (jnp.maximum(i - 1, 0) // seq_tiles, 0, 0))]
        out_shape = [jax.ShapeDtypeStruct((m, D_MODEL), jnp.float32),
                     jax.ShapeDtypeStruct((m // seq_len, CONV_WIDTH - 1, 2 * D_FF), jnp.float32)]
    in_specs += [
        pl.BlockSpec((1, SB_WIDTH), fixed),
        pl.BlockSpec((D_MODEL, D_MODEL), fixed, **once),
        pl.BlockSpec((1, D_MODEL), fixed),
        pl.BlockSpec((1, D_MODEL), fixed),
        pl.BlockSpec((D_MODEL, 2 * D_FF), fixed, **once),
        pl.BlockSpec((CONV_WIDTH, 2 * D_FF), fixed),
        pl.BlockSpec((1, 2 * D_FF), fixed),
        pl.BlockSpec((D_FF, D_MODEL), fixed, **once),
        pl.BlockSpec((1, D_MODEL), fixed),
    ]
    args += [g_b, w_out_b, g_pm, g_pf, w_up_b, conv_w, conv_b, w_dn_b, g_po]
    return pl.pallas_call(
        functools.partial(_mix_ffn_kernel, seq_tiles=seq_tiles, has_hist=has_hist),
        grid=(n_tiles + 1,),
        in_specs=in_specs,
        out_specs=out_specs,
        out_shape=out_shape,
        scratch_shapes=[pltpu.VMEM((tm + V7X_SUBLANES, 2 * D_FF), jnp.float32),
                        pltpu.VMEM((2, tm, D_MODEL), jnp.float32),
                        pltpu.VMEM((2, tm, D_MODEL), jnp.bfloat16)],
        compiler_params=pltpu.CompilerParams(
            dimension_semantics=("arbitrary",), vmem_limit_bytes=V7X_VMEM_LIMIT_BYTES),
        name="mix_ffn",
    )(*args)


def _suffix_matrix():
    r = lax.broadcasted_iota(jnp.int32, (KEY_SUB, KEY_SUB), 0)
    c = lax.broadcasted_iota(jnp.int32, (KEY_SUB, KEY_SUB), 1)
    return jnp.where(r > c, -1.0, 0.0).astype(jnp.bfloat16)


def _layer(x, past_k, past_v, conv_hist, p, *, tm_proj, tq, tm_ffn):
    bsz, t, _ = x.shape
    m = bsz * t
    row = lambda a: a.reshape(1, -1)
    bf16 = jnp.bfloat16

    seq = min(t, SGU_CHUNK)
    w_s = jnp.where(jnp.tril(jnp.ones((seq, seq), bool))[None], p["w_spatial"][:, :seq, :seq], 0.0)
    reps = SGU_CHUNK // seq
    wsp = jnp.einsum("ab,hts->hatbs", jnp.eye(reps, dtype=w_s.dtype), w_s)
    wsp = wsp.reshape(SGU_HEADS, SGU_CHUNK, SGU_CHUNK).astype(bf16)
    b_s = jnp.tile(p["b_spatial"][:, :seq].T, (reps, 1))
    bsp = jnp.repeat(b_s, SGU_HEAD_DIM, axis=1)

    va, an, q, kb, vb, k, v = _in_proj(
        x.reshape(m, D_MODEL), row(p["g_pre_mix"]), p["w_in"].astype(bf16), row(p["ln_v_g"]),
        row(p["ln_v_b"]), wsp, bsp, row(p["g_out_a"]), tm=tm_proj)

    past = past_k.shape[1]
    short = SB_HEADS * t == V7X_LANES
    key_pad = KEY_SUB if short else KEY_TILE
    t_keys = -(-(past + t) // key_pad) * key_pad
    fill = jnp.zeros((bsz, t_keys - past - t, SB_WIDTH), bf16)
    cat = lambda old, new: jnp.concatenate(
        [old.reshape(bsz, past, SB_WIDTH).astype(bf16), new.reshape(bsz, t, SB_WIDTH), fill], axis=1)
    q3 = q.reshape(bsz, t, SB_WIDTH)
    if short:
        ob = _sb_attn_short(q3, cat(past_k, kb), cat(past_v, vb), _suffix_matrix(), past=past)
    else:
        ob = _sb_attn(q3, cat(past_k, kb), cat(past_v, vb), _suffix_matrix(), past=past, tq=tq)

    if conv_hist is None:
        hist_taps = None
    else:
        z = jnp.zeros((bsz, t - 1, 2 * D_FF), jnp.float32)
        tap1 = jnp.concatenate([conv_hist[:, 1:], z], axis=1)
        tap2 = jnp.concatenate([conv_hist, z[:, 1:]], axis=1)
        hist_taps = (tap1.reshape(m, 2 * D_FF), tap2.reshape(m, 2 * D_FF))
    y, extra = _mix_ffn(
        x.reshape(m, D_MODEL), an, ob.reshape(m, SB_WIDTH), hist_taps, row(p["g_out_b"]),
        p["w_out"].astype(bf16), row(p["g_post_mix"]), row(p["g_pre_ffn"]), p["w_up"].astype(bf16),
        p["conv_w"], row(p["conv_b"]), p["w_down"].astype(bf16), row(p["g_post_ffn"]),
        tm=tm_ffn, seq_len=t)
    if conv_hist is None:
        new_hist = extra
    else:
        new_hist = extra.reshape(bsz, t, 2 * D_FF)[:, t - (CONV_WIDTH - 1):]
    shape_kv = (bsz, t, SB_HEADS, SB_HEAD_DIM)
    return (y.reshape(bsz, t, D_MODEL), k.reshape(shape_kv), v.reshape(shape_kv),
            va.reshape(bsz, t, SGU_WIDTH), new_hist)


def kernel(x_prompt, x_sample, cache_sb_k, cache_sb_v, cache_ffn_conv, w_in, g_pre_mix, ln_v_g, ln_v_b, w_spatial, b_spatial, g_out_a, g_out_b, w_out, g_post_mix, g_pre_ffn, w_up, conv_w, conv_b, w_down, g_post_ffn):
    names = ("w_in", "g_pre_mix", "ln_v_g", "ln_v_b", "w_spatial", "b_spatial", "g_out_a", "g_out_b",
             "w_out", "g_post_mix", "g_pre_ffn", "w_up", "conv_w", "conv_b", "w_down", "g_post_ffn")
    stacked = (w_in, g_pre_mix, ln_v_g, ln_v_b, w_spatial, b_spatial, g_out_a, g_out_b, w_out,
               g_post_mix, g_pre_ffn, w_up, conv_w, conv_b, w_down, g_post_ffn)
    depth = w_in.shape[0]
    yp, ys = x_prompt, x_sample
    bp = x_prompt.shape[0]
    empty_kv = jnp.zeros((bp, 0, SB_HEADS, SB_HEAD_DIM), x_prompt.dtype)
    outs = [[] for _ in range(7)]
    for l in range(depth):
        p = {n: a[l] for n, a in zip(names, stacked)}
        yp, kp, vp, _, cp = _layer(yp, empty_kv, empty_kv, None, p, tm_proj=512, tq=1024, tm_ffn=256)
        ys, ksm, vsm, vas, cs = _layer(ys, cache_sb_k[l], cache_sb_v[l], cache_ffn_conv[l], p,
                                       tm_proj=512, tq=ys.shape[1], tm_ffn=128)
        for lst, a in zip(outs, (kp, vp, ksm, vsm, vas, cp, cs)):
            lst.append(a)
    return (yp, ys) + tuple(jnp.stack(lst) for lst in outs)
```

```python
import functools
import math

import jax
import jax.numpy as jnp
from jax import lax
from jax.experimental import pallas as pl
from jax.experimental.pallas import tpu as pltpu

D_MODEL = 1024
SGU_WIDTH = 512
SGU_HEADS = 4
SGU_HEAD_DIM = 128
SGU_CHUNK = 128
SB_WIDTH = 512
SB_HEADS = 8
SB_HEAD_DIM = 64
D_FF = 2816
CONV_WIDTH = 3
EPS = 1e-6

V7X_LANES = 128
V7X_SUBLANES = 8
V7X_MXU_DIM = 256
V7X_VMEM_LIMIT_BYTES = 56 * 1024 * 1024

HEADS_PER_BLOCK = V7X_LANES // SB_HEAD_DIM
KEY_SUB = V7X_MXU_DIM
KEY_SUBS_PER_TILE = 4
KEY_TILE = KEY_SUB * KEY_SUBS_PER_TILE
FF_CHUNK = V7X_MXU_DIM
UNREACHABLE = -1e30
GELU_C1 = math.sqrt(2.0 / math.pi)
GELU_C2 = 0.044715 * GELU_C1


def _rms(x, g):
    return x * lax.rsqrt(jnp.mean(x * x, axis=-1, keepdims=True) + EPS) * g


def _gelu(x):
    return jax.nn.gelu(x)


def _dot(a, b):
    return jnp.dot(a, b, preferred_element_type=jnp.float32)


def _in_proj_kernel(x_ref, g_pre_ref, w_in_ref, ln_g_ref, ln_b_ref, wsp_ref, bsp_ref, g_a_ref,
                    va_ref, an_ref, q_ref, kb_ref, vb_ref, k_ref, v_ref, outa_ref, *, q_scale):
    tm = x_ref.shape[0]
    h = _rms(x_ref[...], g_pre_ref[...]).astype(jnp.bfloat16)

    def proj(col):
        return _dot(h, w_in_ref[:, col:col + SGU_WIDTH])

    u = _gelu(proj(0))
    gv = _gelu(proj(SGU_WIDTH))
    mu = jnp.mean(gv, axis=-1, keepdims=True)
    gc = gv - mu
    va = gc * lax.rsqrt(jnp.mean(gc * gc, axis=-1, keepdims=True) + EPS) * ln_g_ref[...] + ln_b_ref[...]
    va_ref[...] = va
    va_b = va.astype(jnp.bfloat16)
    for c in range(tm // SGU_CHUNK):
        r0 = c * SGU_CHUNK
        for hd in range(SGU_HEADS):
            c0 = hd * SGU_HEAD_DIM
            mixed = _dot(wsp_ref[hd], va_b[r0:r0 + SGU_CHUNK, c0:c0 + SGU_HEAD_DIM])
            mixed = mixed + bsp_ref[:, c0:c0 + SGU_HEAD_DIM]
            outa_ref[r0:r0 + SGU_CHUNK, c0:c0 + SGU_HEAD_DIM] = u[r0:r0 + SGU_CHUNK, c0:c0 + SGU_HEAD_DIM] * mixed
    an_ref[...] = _rms(outa_ref[...], g_a_ref[...]).astype(jnp.bfloat16)

    o = 2 * SGU_WIDTH
    q_ref[...] = (proj(o) * q_scale).astype(jnp.bfloat16)
    k = proj(o + SB_WIDTH)
    kb_ref[...] = k.astype(jnp.bfloat16)
    v = proj(o + 2 * SB_WIDTH)
    vb_ref[...] = v.astype(jnp.bfloat16)
    if len(k_ref.shape) == 3:
        k_ref[0] = k.T
        v_ref[0] = v.T
    else:
        k_ref[...] = k
        v_ref[...] = v


def _in_proj(x, g_pre, w_in_b, ln_g, ln_b, wsp, bsp, g_a, *, tm, seq_len):
    m = x.shape[0]
    in_width = w_in_b.shape[1]
    row = lambda i: (i, 0)
    fixed2 = lambda i: (0, 0)
    fixed3 = lambda i: (0, 0, 0)
    f32 = jnp.float32
    bf16 = jnp.bfloat16
    wide = lambda dt: jax.ShapeDtypeStruct((m, SGU_WIDTH), dt)
    blk = pl.BlockSpec((tm, SGU_WIDTH), row)
    if seq_len % tm == 0 and tm % V7X_LANES == 0:
        tiles = seq_len // tm
        kv_shape = jax.ShapeDtypeStruct((m // seq_len, SB_WIDTH, seq_len), f32)
        kv_blk = pl.BlockSpec((1, SB_WIDTH, tm), lambda i: (i // tiles, 0, i % tiles))
    else:
        kv_shape, kv_blk = wide(f32), blk
    return pl.pallas_call(
        functools.partial(_in_proj_kernel, q_scale=SB_HEAD_DIM ** -0.5),
        grid=(m // tm,),
        in_specs=[
            pl.BlockSpec((tm, D_MODEL), row),
            pl.BlockSpec((1, D_MODEL), fixed2),
            pl.BlockSpec((D_MODEL, in_width), fixed2),
            pl.BlockSpec((1, SGU_WIDTH), fixed2),
            pl.BlockSpec((1, SGU_WIDTH), fixed2),
            pl.BlockSpec((SGU_HEADS, SGU_CHUNK, SGU_CHUNK), fixed3),
            pl.BlockSpec((SGU_CHUNK, SGU_WIDTH), fixed2),
            pl.BlockSpec((1, SGU_WIDTH), fixed2),
        ],
        out_specs=[blk] * 5 + [kv_blk] * 2,
        out_shape=[wide(f32), wide(bf16), wide(bf16), wide(bf16), wide(bf16), kv_shape, kv_shape],
        scratch_shapes=[pltpu.VMEM((tm, SGU_WIDTH), f32)],
        compiler_params=pltpu.CompilerParams(
            dimension_semantics=("arbitrary",), vmem_limit_bytes=V7X_VMEM_LIMIT_BYTES),
        name="in_proj",
    )(x, g_pre, w_in_b, ln_g, ln_b, wsp, bsp, g_a)


def _sb_attn_kernel(q_ref, k_ref, v_ref, u_ref, o_ref, acc_ref, carry_ref, *, past, tq):
    i = pl.program_id(2)
    lo = past + i * tq
    hi = lo + tq - 1
    n_full = lo // KEY_TILE
    n_tot = jnp.maximum(hi - 1, 0) // KEY_TILE + 1

    lane = lax.broadcasted_iota(jnp.int32, (tq, V7X_LANES), 1)
    q2 = q_ref[0]
    zero = jnp.zeros_like(q2)
    q_heads = [jnp.where((lane // SB_HEAD_DIM) == hd, q2, zero) for hd in range(HEADS_PER_BLOCK)]
    u_neg = u_ref[...]

    acc_ref[...] = jnp.zeros_like(acc_ref)
    carry_ref[...] = jnp.zeros_like(carry_ref)

    diagonal = tq == KEY_TILE and past % KEY_TILE == 0

    def tile(j, masked):
        for sb in reversed(range(KEY_SUBS_PER_TILE)):
            k0 = pl.multiple_of(j * KEY_TILE + sb * KEY_SUB, KEY_SUB)
            k2 = k_ref[0, pl.ds(k0, KEY_SUB), :]
            v2 = v_ref[0, pl.ds(k0, KEY_SUB), :]
            r0 = sb * KEY_SUB if (masked and diagonal) else 0
            rows = slice(r0, tq)
            if masked:
                q_pos = lo + r0 + lax.broadcasted_iota(jnp.int32, (tq - r0, KEY_SUB), 0)
                k_pos = k0 + lax.broadcasted_iota(jnp.int32, (tq - r0, KEY_SUB), 1)
                reach = k_pos < q_pos
            for hd in range(HEADS_PER_BLOCK):
                z = lax.dot_general(q_heads[hd][rows], k2, (((1,), (1,)), ((), ())),
                                    preferred_element_type=jnp.float32)
                if masked:
                    z = jnp.where(reach, z, UNREACHABLE)
                zb = z.astype(jnp.bfloat16)
                z_pos = jnp.maximum(zb, 0.0)
                z_neg = jnp.minimum(zb, 0.0)
                l = jnp.log(1.0 + jnp.exp(z_neg - z_pos))
                sp = z_pos + l
                own = (z_neg - l).astype(jnp.float32)
                excl = _dot(sp, u_neg)
                carry = carry_ref[hd, rows]
                w = jnp.exp((own + excl + carry).astype(jnp.bfloat16))
                acc_ref[hd, rows] += _dot(w, v2)
                carry_ref[hd, rows] = carry + excl[:, 0:1] - sp[:, 0:1].astype(jnp.float32)

    def masked_body(t, c):
        tile(n_tot - 1 - t, True)
        return c

    def full_body(t, c):
        tile(n_full - 1 - t, False)
        return c

    lax.fori_loop(0, n_tot - n_full, masked_body, 0)
    lax.fori_loop(0, n_full, full_body, 0)

    out = acc_ref[0]
    for hd in range(1, HEADS_PER_BLOCK):
        out = jnp.where((lane // SB_HEAD_DIM) == hd, acc_ref[hd], out)
    o_ref[0] = out


def _sb_attn(q, k_all, v_all, u_neg, *, past, tq):
    b, t, _ = q.shape
    tk = k_all.shape[1]
    n_blk = SB_WIDTH // V7X_LANES
    return pl.pallas_call(
        functools.partial(_sb_attn_kernel, past=past, tq=tq),
        grid=(b, n_blk, t // tq),
        in_specs=[
            pl.BlockSpec((1, tq, V7X_LANES), lambda bi, hp, i: (bi, i, hp)),
            pl.BlockSpec((1, tk, V7X_LANES), lambda bi, hp, i: (bi, 0, hp)),
            pl.BlockSpec((1, tk, V7X_LANES), lambda bi, hp, i: (bi, 0, hp)),
            pl.BlockSpec((KEY_SUB, KEY_SUB), lambda bi, hp, i: (0, 0)),
        ],
        out_specs=pl.BlockSpec((1, tq, V7X_LANES), lambda bi, hp, i: (bi, i, hp)),
        out_shape=jax.ShapeDtypeStruct((b, t, SB_WIDTH), jnp.float32),
        scratch_shapes=[
            pltpu.VMEM((HEADS_PER_BLOCK, tq, V7X_LANES), jnp.float32),
            pltpu.VMEM((HEADS_PER_BLOCK, tq, 1), jnp.float32),
        ],
        compiler_params=pltpu.CompilerParams(
            dimension_semantics=("arbitrary", "arbitrary", "arbitrary"),
            vmem_limit_bytes=V7X_VMEM_LIMIT_BYTES),
        name="sb_attn",
    )(q, k_all, v_all, u_neg)


def _sb_attn_short_kernel(qbd_ref, kc_ref, vc_ref, kn_ref, vn_ref, u_ref, o_ref, *, t):
    past = kc_ref.shape[2]
    n_cache = past // KEY_SUB
    qbd = qbd_ref[0]
    u_neg = u_ref[...]
    fill = jnp.zeros((KEY_SUB - t, SB_WIDTH), jnp.bfloat16)
    nt = (((1,), (1,)), ((), ()))

    def cached(ref, j):
        return ref[0, :, j * KEY_SUB:(j + 1) * KEY_SUB].astype(jnp.bfloat16)

    def fresh(ref):
        return jnp.concatenate([ref[0], fill], axis=0)

    zs = [_dot(qbd, cached(kc_ref, j)) for j in range(n_cache)]
    z_new = lax.dot_general(qbd, fresh(kn_ref), nt, preferred_element_type=jnp.float32)
    key = lax.broadcasted_iota(jnp.int32, (V7X_LANES, KEY_SUB), 1)
    tt = lax.broadcasted_iota(jnp.int32, (V7X_LANES, KEY_SUB), 0) % t
    zs.append(jnp.where(key < tt, z_new, UNREACHABLE))
    expo, tot = [], []
    for z in zs:
        zb = z.astype(jnp.bfloat16)
        l = jnp.log(1.0 + jnp.exp(-jnp.abs(zb)))
        sp = jnp.maximum(zb, 0.0) + l
        own = (jnp.minimum(zb, 0.0) - l).astype(jnp.float32)
        excl = _dot(sp, u_neg)
        expo.append(own + excl)
        tot.append(excl[:, 0:1] - sp[:, 0:1].astype(jnp.float32))
    carry = jnp.zeros((V7X_LANES, 1), jnp.float32)
    acc = jnp.zeros((V7X_LANES, SB_WIDTH), jnp.float32)
    for j in reversed(range(n_cache + 1)):
        w = jnp.exp(expo[j] + carry).astype(jnp.bfloat16)
        if j == n_cache:
            acc = acc + _dot(w, fresh(vn_ref))
        else:
            acc = acc + lax.dot_general(w, cached(vc_ref, j), nt, preferred_element_type=jnp.float32)
        carry = carry + tot[j]
    col_head = lax.broadcasted_iota(jnp.int32, (t, SB_WIDTH), 1) // SB_HEAD_DIM
    out = jnp.zeros((t, SB_WIDTH), jnp.float32)
    for hd in range(SB_HEADS):
        out = jnp.where(col_head == hd, acc[hd * t:(hd + 1) * t, :], out)
    o_ref[0] = out


def _sb_attn_short(q, k_cache_t, v_cache_t, k_new, v_new, u_neg):
    b, t, _ = q.shape
    past = k_cache_t.shape[2]
    assert SB_HEADS * t == V7X_LANES and past % KEY_SUB == 0 and t <= KEY_SUB
    q4 = q.reshape(b, t, SB_HEADS, SB_HEAD_DIM)
    qbd = jnp.einsum("bthd,hg->bhtgd", q4, jnp.eye(SB_HEADS, dtype=q.dtype)).reshape(b, V7X_LANES, SB_WIDTH)
    whole = lambda *shape: pl.BlockSpec((1,) + shape, lambda bi: (bi, 0, 0))
    return pl.pallas_call(
        functools.partial(_sb_attn_short_kernel, t=t),
        grid=(b,),
        in_specs=[
            whole(V7X_LANES, SB_WIDTH),
            whole(SB_WIDTH, past), whole(SB_WIDTH, past), whole(t, SB_WIDTH), whole(t, SB_WIDTH),
            pl.BlockSpec((KEY_SUB, KEY_SUB), lambda bi: (0, 0)),
        ],
        out_specs=whole(t, SB_WIDTH),
        out_shape=jax.ShapeDtypeStruct((b, t, SB_WIDTH), jnp.float32),
        compiler_params=pltpu.CompilerParams(
            dimension_semantics=("arbitrary",), vmem_limit_bytes=V7X_VMEM_LIMIT_BYTES),
        name="sb_attn_short",
    )(qbd, k_cache_t, v_cache_t, k_new, v_new, u_neg)


def _mix_ffn_kernel(*refs, seq_tiles, has_hist):
    if has_hist:
        (x_ref, an_ref, ob_ref, h1_ref, h2_ref, g_b_ref, w_out_ref, g_pm_ref, g_pf_ref, w_up_ref,
         cw_ref, cb_ref, w_dn_ref, g_po_ref, y_ref, up_ref, buf_ref, x1_buf, hn_buf) = refs
    else:
        (x_ref, an_ref, ob_ref, g_b_ref, w_out_ref, g_pm_ref, g_pf_ref, w_up_ref,
         cw_ref, cb_ref, w_dn_ref, g_po_ref, y_ref, hist_ref, buf_ref, x1_buf, hn_buf) = refs
    tm = x_ref.shape[0]
    pad = V7X_SUBLANES
    step = pl.program_id(0)
    slot = step % 2
    prev = 1 - slot

    @pl.when(step == 0)
    def _():
        x1_buf[1] = jnp.zeros(x1_buf.shape[1:], x1_buf.dtype)
        hn_buf[1] = jnp.zeros(hn_buf.shape[1:], hn_buf.dtype)

    x1 = x1_buf[prev]
    h2 = hn_buf[prev]

    def mixer_epilogue():
        bn = _rms(ob_ref[...], g_b_ref[...]).astype(jnp.bfloat16)
        mix = _dot(an_ref[...], w_out_ref[0:SGU_WIDTH, :]) + _dot(bn, w_out_ref[SGU_WIDTH:, :])
        x1_new = x_ref[...] + _rms(mix, g_pm_ref[...])
        x1_buf[slot] = x1_new
        hn_buf[slot] = _rms(x1_new, g_pf_ref[...]).astype(jnp.bfloat16)

    taps = range(CONV_WIDTH)

    def head_rows(s):
        return buf_ref.at[s, pad:pad + s, :]

    def zero_head_rows():
        for s in taps[1:]:
            head_rows(s)[...] = jnp.zeros((s, buf_ref.shape[2]), jnp.float32)

    if has_hist:
        t_in_seq = lax.broadcasted_iota(jnp.int32, (tm, FF_CHUNK), 0) % seq_tiles
        zero_head_rows()
    else:
        @pl.when(step == 0)
        def _():
            for s in taps[1:]:
                buf_ref[s, pad + tm:pad + tm + s, :] = jnp.zeros((s, buf_ref.shape[2]), jnp.float32)

        for s in taps[1:]:
            head_rows(s)[...] = buf_ref[s, pad + tm:pad + tm + s, :]
        pl.when((step - 1) % seq_tiles == 0)(zero_head_rows)

    def up_dot(col):
        up = _dot(h2, w_up_ref[:, col:col + FF_CHUNK])
        for s in taps:
            buf_ref[s, pad + s:pad + s + tm, col:col + FF_CHUNK] = up

    def conv(col, scale):
        up, m1, m2 = (buf_ref[s, pad:pad + tm, col:col + FF_CHUNK] for s in taps)
        if has_hist:
            up_ref[:, col:col + FF_CHUNK] = up
            m1 = jnp.where(t_in_seq >= 1, m1, 0.0) + h1_ref[:, col:col + FF_CHUNK]
            m2 = jnp.where(t_in_seq >= 2, m2, 0.0) + h2_ref[:, col:col + FF_CHUNK]
        else:
            hist_ref[0, :, col:col + FF_CHUNK] = up[tm - 2:tm, :]
        w = cw_ref[:, col:col + FF_CHUNK] * scale
        b = cb_ref[:, col:col + FF_CHUNK] * scale
        return m2 * w[0:1] + m1 * w[1:2] + up * w[2:3] + b

    n_chunks = D_FF // FF_CHUNK
    up_dot(0)
    up_dot(D_FF)
    mixer_epilogue()
    f = jnp.zeros((tm, D_MODEL), jnp.float32)
    prev_act = None
    for c in range(n_chunks):
        col = c * FF_CHUNK
        if c + 1 < n_chunks:
            up_dot(col + FF_CHUNK)
            up_dot(D_FF + col + FF_CHUNK)
        if prev_act is not None:
            f = f + _dot(prev_act, w_dn_ref[col - FF_CHUNK:col, :])
        gate = conv(col, 1.0)
        half_val = conv(D_FF + col, 0.5)
        inner = gate * (GELU_C1 + GELU_C2 * (gate * gate))
        prev_act = ((gate * half_val) * (1.0 + jnp.tanh(inner))).astype(jnp.bfloat16)
    f = f + _dot(prev_act, w_dn_ref[D_FF - FF_CHUNK:D_FF, :])
    y_ref[...] = x1 + _rms(f, g_po_ref[...])


def _mix_ffn(x, an, ob, hist_taps, g_b, w_out_b, g_pm, g_pf, w_up_b, conv_w, conv_b, w_dn_b, g_po,
             *, tm, seq_len):
    m = x.shape[0]
    n_tiles = m // tm
    row = lambda i: (jnp.minimum(i, n_tiles - 1), 0)
    done = lambda i: (jnp.maximum(i - 1, 0), 0)
    fixed = lambda i: (0, 0)
    has_hist = hist_taps is not None
    once = dict(pipeline_mode=pl.Buffered(1))
    in_specs = [
        pl.BlockSpec((tm, D_MODEL), row),
        pl.BlockSpec((tm, SGU_WIDTH), row),
        pl.BlockSpec((tm, SB_WIDTH), row),
    ]
    args = [x, an, ob]
    if has_hist:
        assert tm % seq_len == 0
        seq_tiles = seq_len
        in_specs += [pl.BlockSpec((tm, 2 * D_FF), done)] * 2
        args += list(hist_taps)
        out_specs = [pl.BlockSpec((tm, D_MODEL), done), pl.BlockSpec((tm, 2 * D_FF), done)]
        out_shape = [jax.ShapeDtypeStruct((m, D_MODEL), jnp.float32),
                     jax.ShapeDtypeStruct((m, 2 * D_FF), jnp.float32)]
    else:
        assert seq_len % tm == 0
        seq_tiles = seq_len // tm
        out_specs = [pl.BlockSpec((tm, D_MODEL), done),
                     pl.BlockSpec((1, CONV_WIDTH - 1, 2 * D_FF),
                                  lambda i: (jnp.maximum(i - 1, 0) // seq_tiles, 0, 0))]
        out_shape = [jax.ShapeDtypeStruct((m, D_MODEL), jnp.float32),
                     jax.ShapeDtypeStruct((m // seq_len, CONV_WIDTH - 1, 2 * D_FF), jnp.float32)]
    in_specs += [
        pl.BlockSpec((1, SB_WIDTH), fixed),
        pl.BlockSpec((D_MODEL, D_MODEL), fixed, **once),
        pl.BlockSpec((1, D_MODEL), fixed),
        pl.BlockSpec((1, D_MODEL), fixed),
        pl.BlockSpec((D_MODEL, 2 * D_FF), fixed, **once),
        pl.BlockSpec((CONV_WIDTH, 2 * D_FF), fixed),
        pl.BlockSpec((1, 2 * D_FF), fixed),
        pl.BlockSpec((D_FF, D_MODEL), fixed, **once),
        pl.BlockSpec((1, D_MODEL), fixed),
    ]
    args += [g_b, w_out_b, g_pm, g_pf, w_up_b, conv_w, conv_b, w_dn_b, g_po]
    return pl.pallas_call(
        functools.partial(_mix_ffn_kernel, seq_tiles=seq_tiles, has_hist=has_hist),
        grid=(n_tiles + 1,),
        in_specs=in_specs,
        out_specs=out_specs,
        out_shape=out_shape,
        scratch_shapes=[pltpu.VMEM((CONV_WIDTH, tm + 2 * V7X_SUBLANES, 2 * D_FF), jnp.float32),
                        pltpu.VMEM((2, tm, D_MODEL), jnp.float32),
                        pltpu.VMEM((2, tm, D_MODEL), jnp.bfloat16)],
        compiler_params=pltpu.CompilerParams(
            dimension_semantics=("arbitrary",), vmem_limit_bytes=V7X_VMEM_LIMIT_BYTES),
        name="mix_ffn",
    )(*args)


def _suffix_matrix():
    r = lax.broadcasted_iota(jnp.int32, (KEY_SUB, KEY_SUB), 0)
    c = lax.broadcasted_iota(jnp.int32, (KEY_SUB, KEY_SUB), 1)
    return jnp.where(r > c, -1.0, 0.0).astype(jnp.bfloat16)


def _layer(x, past_k, past_v, conv_hist, p, *, tm_proj, tq, tm_ffn):
    bsz, t, _ = x.shape
    m = bsz * t
    row = lambda a: a.reshape(1, -1)
    bf16 = jnp.bfloat16

    seq = min(t, SGU_CHUNK)
    w_s = jnp.where(jnp.tril(jnp.ones((seq, seq), bool))[None], p["w_spatial"][:, :seq, :seq], 0.0)
    reps = SGU_CHUNK // seq
    wsp = jnp.einsum("ab,hts->hatbs", jnp.eye(reps, dtype=w_s.dtype), w_s)
    wsp = wsp.reshape(SGU_HEADS, SGU_CHUNK, SGU_CHUNK).astype(bf16)
    b_s = jnp.tile(p["b_spatial"][:, :seq].T, (reps, 1))
    bsp = jnp.repeat(b_s, SGU_HEAD_DIM, axis=1)

    va, an, q, kb, vb, k, v = _in_proj(
        x.reshape(m, D_MODEL), row(p["g_pre_mix"]), p["w_in"].astype(bf16), row(p["ln_v_g"]),
        row(p["ln_v_b"]), wsp, bsp, row(p["g_out_a"]), tm=tm_proj, seq_len=t)

    past = past_k.shape[1]
    q3 = q.reshape(bsz, t, SB_WIDTH)
    k3 = kb.reshape(bsz, t, SB_WIDTH)
    v3 = vb.reshape(bsz, t, SB_WIDTH)
    if SB_HEADS * t == V7X_LANES and past % KEY_SUB == 0:
        by_key = lambda c: jnp.transpose(c, (0, 2, 3, 1)).reshape(bsz, SB_WIDTH, past)
        ob = _sb_attn_short(q3, by_key(past_k), by_key(past_v), k3, v3, _suffix_matrix())
    else:
        t_keys = -(-(past + t) // KEY_TILE) * KEY_TILE
        fill = jnp.zeros((bsz, t_keys - past - t, SB_WIDTH), bf16)
        cat = lambda old, new: jnp.concatenate(
            [old.reshape(bsz, past, SB_WIDTH).astype(bf16), new, fill], axis=1)
        ob = _sb_attn(q3, cat(past_k, k3), cat(past_v, v3), _suffix_matrix(), past=past, tq=tq)

    if conv_hist is None:
        hist_taps = None
    else:
        z = jnp.zeros((bsz, t - 1, 2 * D_FF), jnp.float32)
        tap1 = jnp.concatenate([conv_hist[:, 1:], z], axis=1)
        tap2 = jnp.concatenate([conv_hist, z[:, 1:]], axis=1)
        hist_taps = (tap1.reshape(m, 2 * D_FF), tap2.reshape(m, 2 * D_FF))
    y, extra = _mix_ffn(
        x.reshape(m, D_MODEL), an, ob.reshape(m, SB_WIDTH), hist_taps, row(p["g_out_b"]),
        p["w_out"].astype(bf16), row(p["g_post_mix"]), row(p["g_pre_ffn"]), p["w_up"].astype(bf16),
        p["conv_w"], row(p["conv_b"]), p["w_down"].astype(bf16), row(p["g_post_ffn"]),
        tm=tm_ffn, seq_len=t)
    if conv_hist is None:
        new_hist = extra
    else:
        new_hist = extra.reshape(bsz, t, 2 * D_FF)[:, t - (CONV_WIDTH - 1):]
    if k.ndim == 3:
        heads_last = lambda a: jnp.transpose(a.reshape(bsz, SB_HEADS, SB_HEAD_DIM, t), (0, 3, 1, 2))
    else:
        heads_last = lambda a: a.reshape(bsz, t, SB_HEADS, SB_HEAD_DIM)
    return (y.reshape(bsz, t, D_MODEL), heads_last(k), heads_last(v),
            va.reshape(bsz, t, SGU_WIDTH), new_hist)


def kernel(x_prompt, x_sample, cache_sb_k, cache_sb_v, cache_ffn_conv, w_in, g_pre_mix, ln_v_g, ln_v_b, w_spatial, b_spatial, g_out_a, g_out_b, w_out, g_post_mix, g_pre_ffn, w_up, conv_w, conv_b, w_down, g_post_ffn):
    names = ("w_in", "g_pre_mix", "ln_v_g", "ln_v_b", "w_spatial", "b_spatial", "g_out_a", "g_out_b",
             "w_out", "g_post_mix", "g_pre_ffn", "w_up", "conv_w", "conv_b", "w_down", "g_post_ffn")
    stacked = (w_in, g_pre_mix, ln_v_g, ln_v_b, w_spatial, b_spatial, g_out_a, g_out_b, w_out,
               g_post_mix, g_pre_ffn, w_up, conv_w, conv_b, w_down, g_post_ffn)
    depth = w_in.shape[0]
    yp, ys = x_prompt, x_sample
    bp = x_prompt.shape[0]
    empty_kv = jnp.zeros((bp, 0, SB_HEADS, SB_HEAD_DIM), x_prompt.dtype)
    outs = [[] for _ in range(7)]
    for l in range(depth):
        p = {n: a[l] for n, a in zip(names, stacked)}
        yp, kp, vp, _, cp = _layer(yp, empty_kv, empty_kv, None, p, tm_proj=512, tq=1024, tm_ffn=256)
        ys, ksm, vsm, vas, cs = _layer(ys, cache_sb_k[l], cache_sb_v[l], cache_ffn_conv[l], p,
                                       tm_proj=512, tq=ys.shape[1], tm_ffn=128)
        for lst, a in zip(outs, (kp, vp, ksm, vsm, vas, cp, cs)):
            lst.append(a)
    return (yp, ys) + tuple(jnp.stack(lst) for lst in outs)
```

```python
import functools
import math

import jax
import jax.numpy as jnp
from jax import lax
from jax.experimental import pallas as pl
from jax.experimental.pallas import tpu as pltpu

D_MODEL = 1024
SGU_WIDTH = 512
SGU_HEADS = 4
SGU_HEAD_DIM = 128
SGU_CHUNK = 128
SB_WIDTH = 512
SB_HEADS = 8
SB_HEAD_DIM = 64
D_FF = 2816
CONV_WIDTH = 3
EPS = 1e-6

V7X_LANES = 128
V7X_SUBLANES = 8
V7X_MXU_DIM = 256
V7X_VMEM_LIMIT_BYTES = 56 * 1024 * 1024

HEADS_PER_BLOCK = V7X_LANES // SB_HEAD_DIM
KEY_SUB = V7X_MXU_DIM
KEY_SUBS_PER_TILE = 4
KEY_TILE = KEY_SUB * KEY_SUBS_PER_TILE
FF_CHUNK = V7X_MXU_DIM
SKEW = 2
UNREACHABLE = -1e30
GELU_C1 = math.sqrt(2.0 / math.pi)
GELU_C2 = 0.044715 * GELU_C1


def _rms(x, g):
    return x * lax.rsqrt(jnp.mean(x * x, axis=-1, keepdims=True) + EPS) * g


def _gelu(x):
    return jax.nn.gelu(x)


def _dot(a, b):
    return jnp.dot(a, b, preferred_element_type=jnp.float32)


def _in_proj_kernel(x_ref, g_pre_ref, w_in_ref, ln_g_ref, ln_b_ref, wsp_ref, bsp_ref, g_a_ref,
                    va_ref, an_ref, q_ref, kb_ref, vb_ref, k_ref, v_ref, outa_ref, *, q_scale):
    tm = x_ref.shape[0]
    h = _rms(x_ref[...], g_pre_ref[...]).astype(jnp.bfloat16)

    def proj(col):
        return _dot(h, w_in_ref[:, col:col + SGU_WIDTH])

    u = _gelu(proj(0))
    gv = _gelu(proj(SGU_WIDTH))
    mu = jnp.mean(gv, axis=-1, keepdims=True)
    gc = gv - mu
    va = gc * lax.rsqrt(jnp.mean(gc * gc, axis=-1, keepdims=True) + EPS) * ln_g_ref[...] + ln_b_ref[...]
    va_ref[...] = va
    va_b = va.astype(jnp.bfloat16)
    for c in range(tm // SGU_CHUNK):
        r0 = c * SGU_CHUNK
        for hd in range(SGU_HEADS):
            c0 = hd * SGU_HEAD_DIM
            mixed = _dot(wsp_ref[hd], va_b[r0:r0 + SGU_CHUNK, c0:c0 + SGU_HEAD_DIM])
            mixed = mixed + bsp_ref[:, c0:c0 + SGU_HEAD_DIM]
            outa_ref[r0:r0 + SGU_CHUNK, c0:c0 + SGU_HEAD_DIM] = u[r0:r0 + SGU_CHUNK, c0:c0 + SGU_HEAD_DIM] * mixed
    an_ref[...] = _rms(outa_ref[...], g_a_ref[...]).astype(jnp.bfloat16)

    o = 2 * SGU_WIDTH
    q_ref[...] = (proj(o) * q_scale).astype(jnp.bfloat16)
    k = proj(o + SB_WIDTH)
    kb_ref[...] = k.astype(jnp.bfloat16)
    v = proj(o + 2 * SB_WIDTH)
    vb_ref[...] = v.astype(jnp.bfloat16)
    if len(k_ref.shape) == 3:
        k_ref[0] = k.T
        v_ref[0] = v.T
    else:
        k_ref[...] = k
        v_ref[...] = v


def _in_proj(x, g_pre, w_in_b, ln_g, ln_b, wsp, bsp, g_a, *, tm, seq_len):
    m = x.shape[0]
    in_width = w_in_b.shape[1]
    row = lambda i: (i, 0)
    fixed2 = lambda i: (0, 0)
    fixed3 = lambda i: (0, 0, 0)
    f32 = jnp.float32
    bf16 = jnp.bfloat16
    wide = lambda dt: jax.ShapeDtypeStruct((m, SGU_WIDTH), dt)
    blk = pl.BlockSpec((tm, SGU_WIDTH), row)
    if seq_len % tm == 0 and tm % V7X_LANES == 0:
        tiles = seq_len // tm
        kv_shape = jax.ShapeDtypeStruct((m // seq_len, SB_WIDTH, seq_len), f32)
        kv_blk = pl.BlockSpec((1, SB_WIDTH, tm), lambda i: (i // tiles, 0, i % tiles))
    else:
        kv_shape, kv_blk = wide(f32), blk
    return pl.pallas_call(
        functools.partial(_in_proj_kernel, q_scale=SB_HEAD_DIM ** -0.5),
        grid=(m // tm,),
        in_specs=[
            pl.BlockSpec((tm, D_MODEL), row),
            pl.BlockSpec((1, D_MODEL), fixed2),
            pl.BlockSpec((D_MODEL, in_width), fixed2),
            pl.BlockSpec((1, SGU_WIDTH), fixed2),
            pl.BlockSpec((1, SGU_WIDTH), fixed2),
            pl.BlockSpec((SGU_HEADS, SGU_CHUNK, SGU_CHUNK), fixed3),
            pl.BlockSpec((SGU_CHUNK, SGU_WIDTH), fixed2),
            pl.BlockSpec((1, SGU_WIDTH), fixed2),
        ],
        out_specs=[blk] * 5 + [kv_blk] * 2,
        out_shape=[wide(f32), wide(bf16), wide(bf16), wide(bf16), wide(bf16), kv_shape, kv_shape],
        scratch_shapes=[pltpu.VMEM((tm, SGU_WIDTH), f32)],
        compiler_params=pltpu.CompilerParams(
            dimension_semantics=("arbitrary",), vmem_limit_bytes=V7X_VMEM_LIMIT_BYTES),
        name="in_proj",
    )(x, g_pre, w_in_b, ln_g, ln_b, wsp, bsp, g_a)


def _sb_attn_kernel(q_ref, k_ref, v_ref, u_ref, o_ref, acc_ref, carry_ref, *, past, tq):
    i = pl.program_id(2)
    lo = past + i * tq
    hi = lo + tq - 1
    n_full = lo // KEY_TILE
    n_tot = jnp.maximum(hi - 1, 0) // KEY_TILE + 1

    lane = lax.broadcasted_iota(jnp.int32, (tq, V7X_LANES), 1)
    q2 = q_ref[0]
    zero = jnp.zeros_like(q2)
    q_heads = [jnp.where((lane // SB_HEAD_DIM) == hd, q2, zero) for hd in range(HEADS_PER_BLOCK)]
    u_neg = u_ref[...]

    acc_ref[...] = jnp.zeros_like(acc_ref)
    carry_ref[...] = jnp.zeros_like(carry_ref)

    diagonal = tq == KEY_TILE and past % KEY_TILE == 0

    def tile(j, masked):
        chains = [(sb, hd) for sb in reversed(range(KEY_SUBS_PER_TILE)) for hd in range(HEADS_PER_BLOCK)]

        def span(sb):
            k0 = pl.multiple_of(j * KEY_TILE + sb * KEY_SUB, KEY_SUB)
            r0 = sb * KEY_SUB if (masked and diagonal) else 0
            return k0, r0

        def scores(sb, hd):
            k0, r0 = span(sb)
            k2 = k_ref[0, pl.ds(k0, KEY_SUB), :]
            z = lax.dot_general(q_heads[hd][r0:], k2, (((1,), (1,)), ((), ())),
                                preferred_element_type=jnp.float32)
            if masked:
                q_pos = lo + r0 + lax.broadcasted_iota(jnp.int32, (tq - r0, KEY_SUB), 0)
                k_pos = k0 + lax.broadcasted_iota(jnp.int32, (tq - r0, KEY_SUB), 1)
                z = jnp.where(k_pos < q_pos, z, UNREACHABLE)
            return z

        def suffix(z):
            zb = z.astype(jnp.bfloat16)
            z_pos = jnp.maximum(zb, 0.0)
            z_neg = jnp.minimum(zb, 0.0)
            l = jnp.log(1.0 + jnp.exp(z_neg - z_pos))
            sp = z_pos + l
            excl = _dot(sp, u_neg)
            return z_neg - l, excl, excl[:, 0:1] - sp[:, 0:1].astype(jnp.float32)

        def weigh(sb, hd, own, excl, total):
            k0, r0 = span(sb)
            v2 = v_ref[0, pl.ds(k0, KEY_SUB), :]
            carry = carry_ref[hd, r0:]
            w = jnp.exp((excl + carry).astype(jnp.bfloat16) + own)
            acc_ref[hd, r0:] += _dot(w, v2)
            carry_ref[hd, r0:] = carry + total

        n = len(chains)
        zs, sums = {}, {}
        for step in range(n + SKEW):
            if step < n:
                zs[step] = scores(*chains[step])
            if 0 <= step - 1 < n:
                sums[step - 1] = suffix(zs.pop(step - 1))
            if 0 <= step - SKEW < n:
                weigh(*chains[step - SKEW], *sums.pop(step - SKEW))

    def masked_body(t, c):
        tile(n_tot - 1 - t, True)
        return c

    def full_body(t, c):
        tile(n_full - 1 - t, False)
        return c

    lax.fori_loop(0, n_tot - n_full, masked_body, 0)
    lax.fori_loop(0, n_full, full_body, 0)

    out = acc_ref[0]
    for hd in range(1, HEADS_PER_BLOCK):
        out = jnp.where((lane // SB_HEAD_DIM) == hd, acc_ref[hd], out)
    o_ref[0] = out


def _sb_attn(q, k_all, v_all, u_neg, *, past, tq):
    b, t, _ = q.shape
    tk = k_all.shape[1]
    n_blk = SB_WIDTH // V7X_LANES
    return pl.pallas_call(
        functools.partial(_sb_attn_kernel, past=past, tq=tq),
        grid=(b, n_blk, t // tq),
        in_specs=[
            pl.BlockSpec((1, tq, V7X_LANES), lambda bi, hp, i: (bi, i, hp)),
            pl.BlockSpec((1, tk, V7X_LANES), lambda bi, hp, i: (bi, 0, hp)),
            pl.BlockSpec((1, tk, V7X_LANES), lambda bi, hp, i: (bi, 0, hp)),
            pl.BlockSpec((KEY_SUB, KEY_SUB), lambda bi, hp, i: (0, 0)),
        ],
        out_specs=pl.BlockSpec((1, tq, V7X_LANES), lambda bi, hp, i: (bi, i, hp)),
        out_shape=jax.ShapeDtypeStruct((b, t, SB_WIDTH), jnp.float32),
        scratch_shapes=[
            pltpu.VMEM((HEADS_PER_BLOCK, tq, V7X_LANES), jnp.float32),
            pltpu.VMEM((HEADS_PER_BLOCK, tq, 1), jnp.float32),
        ],
        compiler_params=pltpu.CompilerParams(
            dimension_semantics=("arbitrary", "arbitrary", "arbitrary"),
            vmem_limit_bytes=V7X_VMEM_LIMIT_BYTES),
        name="sb_attn",
    )(q, k_all, v_all, u_neg)


def _sb_attn_short_kernel(qbd_ref, kc_ref, vc_ref, kn_ref, vn_ref, u_ref, o_ref, *, t):
    past = kc_ref.shape[2]
    n_cache = past // KEY_SUB
    qbd = qbd_ref[0]
    u_neg = u_ref[...]
    fill = jnp.zeros((KEY_SUB - t, SB_WIDTH), jnp.bfloat16)
    nt = (((1,), (1,)), ((), ()))

    def cached(ref, j):
        return ref[0, :, j * KEY_SUB:(j + 1) * KEY_SUB].astype(jnp.bfloat16)

    def fresh(ref):
        return jnp.concatenate([ref[0], fill], axis=0)

    zs = [_dot(qbd, cached(kc_ref, j)) for j in range(n_cache)]
    z_new = lax.dot_general(qbd, fresh(kn_ref), nt, preferred_element_type=jnp.float32)
    key = lax.broadcasted_iota(jnp.int32, (V7X_LANES, KEY_SUB), 1)
    tt = lax.broadcasted_iota(jnp.int32, (V7X_LANES, KEY_SUB), 0) % t
    zs.append(jnp.where(key < tt, z_new, UNREACHABLE))
    expo, tot = [], []
    for z in zs:
        zb = z.astype(jnp.bfloat16)
        l = jnp.log(1.0 + jnp.exp(-jnp.abs(zb)))
        sp = jnp.maximum(zb, 0.0) + l
        own = (jnp.minimum(zb, 0.0) - l).astype(jnp.float32)
        excl = _dot(sp, u_neg)
        expo.append(own + excl)
        tot.append(excl[:, 0:1] - sp[:, 0:1].astype(jnp.float32))
    carry = jnp.zeros((V7X_LANES, 1), jnp.float32)
    acc = jnp.zeros((V7X_LANES, SB_WIDTH), jnp.float32)
    for j in reversed(range(n_cache + 1)):
        w = jnp.exp(expo[j] + carry).astype(jnp.bfloat16)
        if j == n_cache:
            acc = acc + _dot(w, fresh(vn_ref))
        else:
            acc = acc + lax.dot_general(w, cached(vc_ref, j), nt, preferred_element_type=jnp.float32)
        carry = carry + tot[j]
    col_head = lax.broadcasted_iota(jnp.int32, (t, SB_WIDTH), 1) // SB_HEAD_DIM
    out = jnp.zeros((t, SB_WIDTH), jnp.float32)
    for hd in range(SB_HEADS):
        out = jnp.where(col_head == hd, acc[hd * t:(hd + 1) * t, :], out)
    o_ref[0] = out


def _sb_attn_short(q, k_cache_t, v_cache_t, k_new, v_new, u_neg):
    b, t, _ = q.shape
    past = k_cache_t.shape[2]
    assert SB_HEADS * t == V7X_LANES and past % KEY_SUB == 0 and t <= KEY_SUB
    q4 = q.reshape(b, t, SB_HEADS, SB_HEAD_DIM)
    qbd = jnp.einsum("bthd,hg->bhtgd", q4, jnp.eye(SB_HEADS, dtype=q.dtype)).reshape(b, V7X_LANES, SB_WIDTH)
    whole = lambda *shape: pl.BlockSpec((1,) + shape, lambda bi: (bi, 0, 0))
    return pl.pallas_call(
        functools.partial(_sb_attn_short_kernel, t=t),
        grid=(b,),
        in_specs=[
            whole(V7X_LANES, SB_WIDTH),
            whole(SB_WIDTH, past), whole(SB_WIDTH, past), whole(t, SB_WIDTH), whole(t, SB_WIDTH),
            pl.BlockSpec((KEY_SUB, KEY_SUB), lambda bi: (0, 0)),
        ],
        out_specs=whole(t, SB_WIDTH),
        out_shape=jax.ShapeDtypeStruct((b, t, SB_WIDTH), jnp.float32),
        compiler_params=pltpu.CompilerParams(
            dimension_semantics=("arbitrary",), vmem_limit_bytes=V7X_VMEM_LIMIT_BYTES),
        name="sb_attn_short",
    )(qbd, k_cache_t, v_cache_t, k_new, v_new, u_neg)


def _mix_ffn_kernel(*refs, seq_tiles, has_hist):
    if has_hist:
        (x_ref, an_ref, ob_ref, h1_ref, h2_ref, g_b_ref, w_out_ref, g_pm_ref, g_pf_ref, w_up_ref,
         cw_ref, cb_ref, w_dn_ref, g_po_ref, y_ref, up_ref, buf_ref, x1_buf, hn_buf) = refs
    else:
        (x_ref, an_ref, ob_ref, g_b_ref, w_out_ref, g_pm_ref, g_pf_ref, w_up_ref,
         cw_ref, cb_ref, w_dn_ref, g_po_ref, y_ref, hist_ref, buf_ref, x1_buf, hn_buf) = refs
    tm = x_ref.shape[0]
    pad = V7X_SUBLANES
    step = pl.program_id(0)
    slot = step % 2
    prev = 1 - slot

    @pl.when(step == 0)
    def _():
        x1_buf[1] = jnp.zeros(x1_buf.shape[1:], x1_buf.dtype)
        hn_buf[1] = jnp.zeros(hn_buf.shape[1:], hn_buf.dtype)

    x1 = x1_buf[prev]
    h2 = hn_buf[prev]

    def mixer_epilogue():
        bn = _rms(ob_ref[...], g_b_ref[...]).astype(jnp.bfloat16)
        mix = _dot(an_ref[...], w_out_ref[0:SGU_WIDTH, :]) + _dot(bn, w_out_ref[SGU_WIDTH:, :])
        x1_new = x_ref[...] + _rms(mix, g_pm_ref[...])
        x1_buf[slot] = x1_new
        hn_buf[slot] = _rms(x1_new, g_pf_ref[...]).astype(jnp.bfloat16)

    taps = range(CONV_WIDTH)

    def head_rows(s):
        return buf_ref.at[s, pad:pad + s, :]

    def zero_head_rows():
        for s in taps[1:]:
            head_rows(s)[...] = jnp.zeros((s, buf_ref.shape[2]), jnp.float32)

    if has_hist:
        t_in_seq = lax.broadcasted_iota(jnp.int32, (tm, FF_CHUNK), 0) % seq_tiles
        zero_head_rows()
    else:
        @pl.when(step == 0)
        def _():
            for s in taps[1:]:
                buf_ref[s, pad + tm:pad + tm + s, :] = jnp.zeros((s, buf_ref.shape[2]), jnp.float32)

        for s in taps[1:]:
            head_rows(s)[...] = buf_ref[s, pad + tm:pad + tm + s, :]
        pl.when((step - 1) % seq_tiles == 0)(zero_head_rows)

    def up_dot(col):
        up = _dot(h2, w_up_ref[:, col:col + FF_CHUNK])
        for s in taps:
            buf_ref[s, pad + s:pad + s + tm, col:col + FF_CHUNK] = up

    def conv(col, scale):
        up, m1, m2 = (buf_ref[s, pad:pad + tm, col:col + FF_CHUNK] for s in taps)
        if has_hist:
            up_ref[:, col:col + FF_CHUNK] = up
            m1 = jnp.where(t_in_seq >= 1, m1, 0.0) + h1_ref[:, col:col + FF_CHUNK]
            m2 = jnp.where(t_in_seq >= 2, m2, 0.0) + h2_ref[:, col:col + FF_CHUNK]
        else:
            hist_ref[0, :, col:col + FF_CHUNK] = up[tm - 2:tm, :]
        w = cw_ref[:, col:col + FF_CHUNK] * scale
        b = cb_ref[:, col:col + FF_CHUNK] * scale
        return m2 * w[0:1] + m1 * w[1:2] + up * w[2:3] + b

    n_chunks = D_FF // FF_CHUNK
    up_dot(0)
    up_dot(D_FF)
    mixer_epilogue()
    f = jnp.zeros((tm, D_MODEL), jnp.float32)
    prev_act = None
    for c in range(n_chunks):
        col = c * FF_CHUNK
        if c + 1 < n_chunks:
            up_dot(col + FF_CHUNK)
            up_dot(D_FF + col + FF_CHUNK)
        if prev_act is not None:
            f = f + _dot(prev_act, w_dn_ref[col - FF_CHUNK:col, :])
        gate = conv(col, 1.0)
        half_val = conv(D_FF + col, 0.5)
        inner = gate * (GELU_C1 + GELU_C2 * (gate * gate))
        prev_act = ((gate * half_val) * (1.0 + jnp.tanh(inner))).astype(jnp.bfloat16)
    f = f + _dot(prev_act, w_dn_ref[D_FF - FF_CHUNK:D_FF, :])
    y_ref[...] = x1 + _rms(f, g_po_ref[...])


def _mix_ffn(x, an, ob, hist_taps, g_b, w_out_b, g_pm, g_pf, w_up_b, conv_w, conv_b, w_dn_b, g_po,
             *, tm, seq_len):
    m = x.shape[0]
    n_tiles = m // tm
    row = lambda i: (jnp.minimum(i, n_tiles - 1), 0)
    done = lambda i: (jnp.maximum(i - 1, 0), 0)
    fixed = lambda i: (0, 0)
    has_hist = hist_taps is not None
    once = dict(pipeline_mode=pl.Buffered(1))
    in_specs = [
        pl.BlockSpec((tm, D_MODEL), row),
        pl.BlockSpec((tm, SGU_WIDTH), row),
        pl.BlockSpec((tm, SB_WIDTH), row),
    ]
    args = [x, an, ob]
    if has_hist:
        assert tm % seq_len == 0
        seq_tiles = seq_len
        in_specs += [pl.BlockSpec((tm, 2 * D_FF), done)] * 2
        args += list(hist_taps)
        out_specs = [pl.BlockSpec((tm, D_MODEL), done), pl.BlockSpec((tm, 2 * D_FF), done)]
        out_shape = [jax.ShapeDtypeStruct((m, D_MODEL), jnp.float32),
                     jax.ShapeDtypeStruct((m, 2 * D_FF), jnp.float32)]
    else:
        assert seq_len % tm == 0
        seq_tiles = seq_len // tm
        out_specs = [pl.BlockSpec((tm, D_MODEL), done),
                     pl.BlockSpec((1, CONV_WIDTH - 1, 2 * D_FF),
                                  lambda i: (jnp.maximum(i - 1, 0) // seq_tiles, 0, 0))]
        out_shape = [jax.ShapeDtypeStruct((m, D_MODEL), jnp.float32),
                     jax.ShapeDtypeStruct((m // seq_len, CONV_WIDTH - 1, 2 * D_FF), jnp.float32)]
    in_specs += [
        pl.BlockSpec((1, SB_WIDTH), fixed),
        pl.BlockSpec((D_MODEL, D_MODEL), fixed, **once),
        pl.BlockSpec((1, D_MODEL), fixed),
        pl.BlockSpec((1, D_MODEL), fixed),
        pl.BlockSpec((D_MODEL, 2 * D_FF), fixed, **once),
        pl.BlockSpec((CONV_WIDTH, 2 * D_FF), fixed),
        pl.BlockSpec((1, 2 * D_FF), fixed),
        pl.BlockSpec((D_FF, D_MODEL), fixed, **once),
        pl.BlockSpec((1, D_MODEL), fixed),
    ]
    args += [g_b, w_out_b, g_pm, g_pf, w_up_b, conv_w, conv_b, w_dn_b, g_po]
    return pl.pallas_call(
        functools.partial(_mix_ffn_kernel, seq_tiles=seq_tiles, has_hist=has_hist),
        grid=(n_tiles + 1,),
        in_specs=in_specs,
        out_specs=out_specs,
        out_shape=out_shape,
        scratch_shapes=[pltpu.VMEM((CONV_WIDTH, tm + 2 * V7X_SUBLANES, 2 * D_FF), jnp.float32),
                        pltpu.VMEM((2, tm, D_MODEL), jnp.float32),
                        pltpu.VMEM((2, tm, D_MODEL), jnp.bfloat16)],
        compiler_params=pltpu.CompilerParams(
            dimension_semantics=("arbitrary",), vmem_limit_bytes=V7X_VMEM_LIMIT_BYTES),
        name="mix_ffn",
    )(*args)


def _suffix_matrix():
    r = lax.broadcasted_iota(jnp.int32, (KEY_SUB, KEY_SUB), 0)
    c = lax.broadcasted_iota(jnp.int32, (KEY_SUB, KEY_SUB), 1)
    return jnp.where(r > c, -1.0, 0.0).astype(jnp.bfloat16)


def _layer(x, past_k, past_v, conv_hist, p, *, tm_proj, tq, tm_ffn):
    bsz, t, _ = x.shape
    m = bsz * t
    row = lambda a: a.reshape(1, -1)
    bf16 = jnp.bfloat16

    seq = min(t, SGU_CHUNK)
    w_s = jnp.where(jnp.tril(jnp.ones((seq, seq), bool))[None], p["w_spatial"][:, :seq, :seq], 0.0)
    reps = SGU_CHUNK // seq
    wsp = jnp.einsum("ab,hts->hatbs", jnp.eye(reps, dtype=w_s.dtype), w_s)
    wsp = wsp.reshape(SGU_HEADS, SGU_CHUNK, SGU_CHUNK).astype(bf16)
    b_s = jnp.tile(p["b_spatial"][:, :seq].T, (reps, 1))
    bsp = jnp.repeat(b_s, SGU_HEAD_DIM, axis=1)

    va, an, q, kb, vb, k, v = _in_proj(
        x.reshape(m, D_MODEL), row(p["g_pre_mix"]), p["w_in"].astype(bf16), row(p["ln_v_g"]),
        row(p["ln_v_b"]), wsp, bsp, row(p["g_out_a"]), tm=tm_proj, seq_len=t)

    past = past_k.shape[1]
    q3 = q.reshape(bsz, t, SB_WIDTH)
    k3 = kb.reshape(bsz, t, SB_WIDTH)
    v3 = vb.reshape(bsz, t, SB_WIDTH)
    if SB_HEADS * t == V7X_LANES and past % KEY_SUB == 0:
        by_key = lambda c: jnp.transpose(c, (0, 2, 3, 1)).reshape(bsz, SB_WIDTH, past)
        ob = _sb_attn_short(q3, by_key(past_k), by_key(past_v), k3, v3, _suffix_matrix())
    else:
        t_keys = -(-(past + t) // KEY_TILE) * KEY_TILE
        fill = jnp.zeros((bsz, t_keys - past - t, SB_WIDTH), bf16)
        cat = lambda old, new: jnp.concatenate(
            [old.reshape(bsz, past, SB_WIDTH).astype(bf16), new, fill], axis=1)
        ob = _sb_attn(q3, cat(past_k, k3), cat(past_v, v3), _suffix_matrix(), past=past, tq=tq)

    if conv_hist is None:
        hist_taps = None
    else:
        z = jnp.zeros((bsz, t - 1, 2 * D_FF), jnp.float32)
        tap1 = jnp.concatenate([conv_hist[:, 1:], z], axis=1)
        tap2 = jnp.concatenate([conv_hist, z[:, 1:]], axis=1)
        hist_taps = (tap1.reshape(m, 2 * D_FF), tap2.reshape(m, 2 * D_FF))
    y, extra = _mix_ffn(
        x.reshape(m, D_MODEL), an, ob.reshape(m, SB_WIDTH), hist_taps, row(p["g_out_b"]),
        p["w_out"].astype(bf16), row(p["g_post_mix"]), row(p["g_pre_ffn"]), p["w_up"].astype(bf16),
        p["conv_w"], row(p["conv_b"]), p["w_down"].astype(bf16), row(p["g_post_ffn"]),
        tm=tm_ffn, seq_len=t)
    if conv_hist is None:
        new_hist = extra
    else:
        new_hist = extra.reshape(bsz, t, 2 * D_FF)[:, t - (CONV_WIDTH - 1):]
    if k.ndim == 3:
        heads_last = lambda a: jnp.transpose(a.reshape(bsz, SB_HEADS, SB_HEAD_DIM, t), (0, 3, 1, 2))
    else:
        heads_last = lambda a: a.reshape(bsz, t, SB_HEADS, SB_HEAD_DIM)
    return (y.reshape(bsz, t, D_MODEL), heads_last(k), heads_last(v),
            va.reshape(bsz, t, SGU_WIDTH), new_hist)


def kernel(x_prompt, x_sample, cache_sb_k, cache_sb_v, cache_ffn_conv, w_in, g_pre_mix, ln_v_g, ln_v_b, w_spatial, b_spatial, g_out_a, g_out_b, w_out, g_post_mix, g_pre_ffn, w_up, conv_w, conv_b, w_down, g_post_ffn):
    names = ("w_in", "g_pre_mix", "ln_v_g", "ln_v_b", "w_spatial", "b_spatial", "g_out_a", "g_out_b",
             "w_out", "g_post_mix", "g_pre_ffn", "w_up", "conv_w", "conv_b", "w_down", "g_post_ffn")
    stacked = (w_in, g_pre_mix, ln_v_g, ln_v_b, w_spatial, b_spatial, g_out_a, g_out_b, w_out,
               g_post_mix, g_pre_ffn, w_up, conv_w, conv_b, w_down, g_post_ffn)
    depth = w_in.shape[0]
    yp, ys = x_prompt, x_sample
    bp = x_prompt.shape[0]
    empty_kv = jnp.zeros((bp, 0, SB_HEADS, SB_HEAD_DIM), x_prompt.dtype)
    outs = [[] for _ in range(7)]
    for l in range(depth):
        p = {n: a[l] for n, a in zip(names, stacked)}
        yp, kp, vp, _, cp = _layer(yp, empty_kv, empty_kv, None, p, tm_proj=512, tq=1024, tm_ffn=256)
        ys, ksm, vsm, vas, cs = _layer(ys, cache_sb_k[l], cache_sb_v[l], cache_ffn_conv[l], p,
                                       tm_proj=512, tq=ys.shape[1], tm_ffn=128)
        for lst, a in zip(outs, (kp, vp, ksm, vsm, vas, cp, cs)):
            lst.append(a)
    return (yp, ys) + tuple(jnp.stack(lst) for lst in outs)
```

```python
import functools
import math

import jax
import jax.numpy as jnp
from jax import lax
from jax.experimental import pallas as pl
from jax.experimental.pallas import tpu as pltpu

D_MODEL = 1024
SGU_WIDTH = 512
SGU_HEADS = 4
SGU_HEAD_DIM = 128
SGU_CHUNK = 128
SB_WIDTH = 512
SB_HEADS = 8
SB_HEAD_DIM = 64
D_FF = 2816
CONV_WIDTH = 3
EPS = 1e-6

V7X_LANES = 128
V7X_SUBLANES = 8
V7X_MXU_DIM = 256
V7X_VMEM_LIMIT_BYTES = 56 * 1024 * 1024

HEADS_PER_BLOCK = V7X_LANES // SB_HEAD_DIM
KEY_SUB = V7X_MXU_DIM
KEY_SUBS_PER_TILE = 4
KEY_TILE = KEY_SUB * KEY_SUBS_PER_TILE
FF_CHUNK = V7X_MXU_DIM
SKEW = 2
UNREACHABLE = -1e30
GELU_C1 = math.sqrt(2.0 / math.pi)
GELU_C2 = 0.044715 * GELU_C1


def _rms(x, g):
    return x * lax.rsqrt(jnp.mean(x * x, axis=-1, keepdims=True) + EPS) * g


def _gelu(x):
    return jax.nn.gelu(x)


def _dot(a, b):
    return jnp.dot(a, b, preferred_element_type=jnp.float32)


def _in_proj_kernel(x_ref, g_pre_ref, w_in_ref, ln_g_ref, ln_b_ref, wsp_ref, bsp_ref, g_a_ref,
                    va_ref, an_ref, q_ref, kb_ref, vb_ref, k_ref, v_ref, outa_ref, *, q_scale):
    tm = x_ref.shape[0]
    h = _rms(x_ref[...], g_pre_ref[...]).astype(jnp.bfloat16)

    def proj(col):
        return _dot(h, w_in_ref[:, col:col + SGU_WIDTH])

    o = 2 * SGU_WIDTH
    pu = proj(0)
    pg = proj(SGU_WIDTH)
    q_ref[...] = (proj(o) * q_scale).astype(jnp.bfloat16)
    k = proj(o + SB_WIDTH)
    kb_ref[...] = k.astype(jnp.bfloat16)
    v = proj(o + 2 * SB_WIDTH)
    vb_ref[...] = v.astype(jnp.bfloat16)
    if len(k_ref.shape) == 3:
        k_ref[0] = k.T
        v_ref[0] = v.T
    else:
        k_ref[...] = k
        v_ref[...] = v

    u = _gelu(pu)
    gv = _gelu(pg)
    mu = jnp.mean(gv, axis=-1, keepdims=True)
    gc = gv - mu
    va = gc * lax.rsqrt(jnp.mean(gc * gc, axis=-1, keepdims=True) + EPS) * ln_g_ref[...] + ln_b_ref[...]
    va_ref[...] = va
    va_b = va.astype(jnp.bfloat16)
    for c in range(tm // SGU_CHUNK):
        r0 = c * SGU_CHUNK
        for hd in range(SGU_HEADS):
            c0 = hd * SGU_HEAD_DIM
            mixed = _dot(wsp_ref[hd], va_b[r0:r0 + SGU_CHUNK, c0:c0 + SGU_HEAD_DIM])
            mixed = mixed + bsp_ref[:, c0:c0 + SGU_HEAD_DIM]
            outa_ref[r0:r0 + SGU_CHUNK, c0:c0 + SGU_HEAD_DIM] = u[r0:r0 + SGU_CHUNK, c0:c0 + SGU_HEAD_DIM] * mixed
    an_ref[...] = _rms(outa_ref[...], g_a_ref[...]).astype(jnp.bfloat16)


def _in_proj(x, g_pre, w_in_b, ln_g, ln_b, wsp, bsp, g_a, *, tm, seq_len):
    m = x.shape[0]
    in_width = w_in_b.shape[1]
    row = lambda i: (i, 0)
    fixed2 = lambda i: (0, 0)
    fixed3 = lambda i: (0, 0, 0)
    f32 = jnp.float32
    bf16 = jnp.bfloat16
    wide = lambda dt: jax.ShapeDtypeStruct((m, SGU_WIDTH), dt)
    blk = pl.BlockSpec((tm, SGU_WIDTH), row)
    if seq_len % tm == 0 and tm % V7X_LANES == 0:
        tiles = seq_len // tm
        kv_shape = jax.ShapeDtypeStruct((m // seq_len, SB_WIDTH, seq_len), f32)
        kv_blk = pl.BlockSpec((1, SB_WIDTH, tm), lambda i: (i // tiles, 0, i % tiles))
    else:
        kv_shape, kv_blk = wide(f32), blk
    return pl.pallas_call(
        functools.partial(_in_proj_kernel, q_scale=SB_HEAD_DIM ** -0.5),
        grid=(m // tm,),
        in_specs=[
            pl.BlockSpec((tm, D_MODEL), row),
            pl.BlockSpec((1, D_MODEL), fixed2),
            pl.BlockSpec((D_MODEL, in_width), fixed2),
            pl.BlockSpec((1, SGU_WIDTH), fixed2),
            pl.BlockSpec((1, SGU_WIDTH), fixed2),
            pl.BlockSpec((SGU_HEADS, SGU_CHUNK, SGU_CHUNK), fixed3),
            pl.BlockSpec((SGU_CHUNK, SGU_WIDTH), fixed2),
            pl.BlockSpec((1, SGU_WIDTH), fixed2),
        ],
        out_specs=[blk] * 5 + [kv_blk] * 2,
        out_shape=[wide(f32), wide(bf16), wide(bf16), wide(bf16), wide(bf16), kv_shape, kv_shape],
        scratch_shapes=[pltpu.VMEM((tm, SGU_WIDTH), f32)],
        compiler_params=pltpu.CompilerParams(
            dimension_semantics=("arbitrary",), vmem_limit_bytes=V7X_VMEM_LIMIT_BYTES),
        name="in_proj",
    )(x, g_pre, w_in_b, ln_g, ln_b, wsp, bsp, g_a)


def _sb_attn_kernel(q_ref, k_ref, v_ref, u_ref, o_ref, acc_ref, carry_ref, *, past, tq):
    i = pl.program_id(2)
    lo = past + i * tq
    hi = lo + tq - 1
    n_full = lo // KEY_TILE
    n_tot = jnp.maximum(hi - 1, 0) // KEY_TILE + 1

    lane = lax.broadcasted_iota(jnp.int32, (tq, V7X_LANES), 1)
    q2 = q_ref[0]
    zero = jnp.zeros_like(q2)
    q_heads = [jnp.where((lane // SB_HEAD_DIM) == hd, q2, zero) for hd in range(HEADS_PER_BLOCK)]
    u_neg = u_ref[...]

    acc_ref[...] = jnp.zeros_like(acc_ref)
    carry_ref[...] = jnp.zeros_like(carry_ref)

    diagonal = tq == KEY_TILE and past % KEY_TILE == 0

    def tile(j, masked):
        chains = [(sb, hd) for sb in reversed(range(KEY_SUBS_PER_TILE)) for hd in range(HEADS_PER_BLOCK)]

        def span(sb):
            k0 = pl.multiple_of(j * KEY_TILE + sb * KEY_SUB, KEY_SUB)
            r0 = sb * KEY_SUB if (masked and diagonal) else 0
            return k0, r0

        def scores(sb, hd):
            k0, r0 = span(sb)
            k2 = k_ref[0, pl.ds(k0, KEY_SUB), :]
            z = lax.dot_general(q_heads[hd][r0:], k2, (((1,), (1,)), ((), ())),
                                preferred_element_type=jnp.float32)
            if masked:
                q_pos = lo + r0 + lax.broadcasted_iota(jnp.int32, (tq - r0, KEY_SUB), 0)
                k_pos = k0 + lax.broadcasted_iota(jnp.int32, (tq - r0, KEY_SUB), 1)
                z = jnp.where(k_pos < q_pos, z, UNREACHABLE)
            return z

        def suffix(z):
            zb = z.astype(jnp.bfloat16)
            z_pos = jnp.maximum(zb, 0.0)
            z_neg = jnp.minimum(zb, 0.0)
            l = jnp.log(1.0 + jnp.exp(z_neg - z_pos))
            sp = z_pos + l
            excl = _dot(sp, u_neg)
            return z_neg - l, excl, excl[:, 0:1] - sp[:, 0:1].astype(jnp.float32)

        def weigh(sb, hd, own, excl, total):
            k0, r0 = span(sb)
            v2 = v_ref[0, pl.ds(k0, KEY_SUB), :]
            carry = carry_ref[hd, r0:]
            w = jnp.exp((excl + carry).astype(jnp.bfloat16) + own)
            acc_ref[hd, r0:] += _dot(w, v2)
            carry_ref[hd, r0:] = carry + total

        n = len(chains)
        zs, sums = {}, {}
        for step in range(n + SKEW):
            if step < n:
                zs[step] = scores(*chains[step])
            if 0 <= step - 1 < n:
                sums[step - 1] = suffix(zs.pop(step - 1))
            if 0 <= step - SKEW < n:
                weigh(*chains[step - SKEW], *sums.pop(step - SKEW))

    def masked_body(t, c):
        tile(n_tot - 1 - t, True)
        return c

    def full_body(t, c):
        tile(n_full - 1 - t, False)
        return c

    lax.fori_loop(0, n_tot - n_full, masked_body, 0)
    lax.fori_loop(0, n_full, full_body, 0)

    out = acc_ref[0]
    for hd in range(1, HEADS_PER_BLOCK):
        out = jnp.where((lane // SB_HEAD_DIM) == hd, acc_ref[hd], out)
    o_ref[0] = out


def _sb_attn(q, k_all, v_all, u_neg, *, past, tq):
    b, t, _ = q.shape
    tk = k_all.shape[1]
    n_blk = SB_WIDTH // V7X_LANES
    return pl.pallas_call(
        functools.partial(_sb_attn_kernel, past=past, tq=tq),
        grid=(b, n_blk, t // tq),
        in_specs=[
            pl.BlockSpec((1, tq, V7X_LANES), lambda bi, hp, i: (bi, i, hp)),
            pl.BlockSpec((1, tk, V7X_LANES), lambda bi, hp, i: (bi, 0, hp)),
            pl.BlockSpec((1, tk, V7X_LANES), lambda bi, hp, i: (bi, 0, hp)),
            pl.BlockSpec((KEY_SUB, KEY_SUB), lambda bi, hp, i: (0, 0)),
        ],
        out_specs=pl.BlockSpec((1, tq, V7X_LANES), lambda bi, hp, i: (bi, i, hp)),
        out_shape=jax.ShapeDtypeStruct((b, t, SB_WIDTH), jnp.float32),
        scratch_shapes=[
            pltpu.VMEM((HEADS_PER_BLOCK, tq, V7X_LANES), jnp.float32),
            pltpu.VMEM((HEADS_PER_BLOCK, tq, 1), jnp.float32),
        ],
        compiler_params=pltpu.CompilerParams(
            dimension_semantics=("arbitrary", "arbitrary", "arbitrary"),
            vmem_limit_bytes=V7X_VMEM_LIMIT_BYTES),
        name="sb_attn",
    )(q, k_all, v_all, u_neg)


def _sb_attn_short_kernel(qbd_ref, kc_ref, vc_ref, kn_ref, vn_ref, u_ref, o_ref, *, t):
    past = kc_ref.shape[2]
    n_cache = past // KEY_SUB
    qbd = qbd_ref[0]
    u_neg = u_ref[...]
    fill = jnp.zeros((KEY_SUB - t, SB_WIDTH), jnp.bfloat16)
    nt = (((1,), (1,)), ((), ()))

    def cached(ref, j):
        return ref[0, :, j * KEY_SUB:(j + 1) * KEY_SUB].astype(jnp.bfloat16)

    def fresh(ref):
        return jnp.concatenate([ref[0], fill], axis=0)

    zs = [_dot(qbd, cached(kc_ref, j)) for j in range(n_cache)]
    z_new = lax.dot_general(qbd, fresh(kn_ref), nt, preferred_element_type=jnp.float32)
    key = lax.broadcasted_iota(jnp.int32, (V7X_LANES, KEY_SUB), 1)
    tt = lax.broadcasted_iota(jnp.int32, (V7X_LANES, KEY_SUB), 0) % t
    zs.append(jnp.where(key < tt, z_new, UNREACHABLE))
    expo, tot = [], []
    for z in zs:
        zb = z.astype(jnp.bfloat16)
        l = jnp.log(1.0 + jnp.exp(-jnp.abs(zb)))
        sp = jnp.maximum(zb, 0.0) + l
        own = (jnp.minimum(zb, 0.0) - l).astype(jnp.float32)
        excl = _dot(sp, u_neg)
        expo.append(own + excl)
        tot.append(excl[:, 0:1] - sp[:, 0:1].astype(jnp.float32))
    carry = jnp.zeros((V7X_LANES, 1), jnp.float32)
    acc = jnp.zeros((V7X_LANES, SB_WIDTH), jnp.float32)
    for j in reversed(range(n_cache + 1)):
        w = jnp.exp(expo[j] + carry).astype(jnp.bfloat16)
        if j == n_cache:
            acc = acc + _dot(w, fresh(vn_ref))
        else:
            acc = acc + lax.dot_general(w, cached(vc_ref, j), nt, preferred_element_type=jnp.float32)
        carry = carry + tot[j]
    col_head = lax.broadcasted_iota(jnp.int32, (t, SB_WIDTH), 1) // SB_HEAD_DIM
    out = jnp.zeros((t, SB_WIDTH), jnp.float32)
    for hd in range(SB_HEADS):
        out = jnp.where(col_head == hd, acc[hd * t:(hd + 1) * t, :], out)
    o_ref[0] = out


def _sb_attn_short(q, k_cache_t, v_cache_t, k_new, v_new, u_neg):
    b, t, _ = q.shape
    past = k_cache_t.shape[2]
    assert SB_HEADS * t == V7X_LANES and past % KEY_SUB == 0 and t <= KEY_SUB
    q4 = q.reshape(b, t, SB_HEADS, SB_HEAD_DIM)
    qbd = jnp.einsum("bthd,hg->bhtgd", q4, jnp.eye(SB_HEADS, dtype=q.dtype)).reshape(b, V7X_LANES, SB_WIDTH)
    whole = lambda *shape: pl.BlockSpec((1,) + shape, lambda bi: (bi, 0, 0))
    return pl.pallas_call(
        functools.partial(_sb_attn_short_kernel, t=t),
        grid=(b,),
        in_specs=[
            whole(V7X_LANES, SB_WIDTH),
            whole(SB_WIDTH, past), whole(SB_WIDTH, past), whole(t, SB_WIDTH), whole(t, SB_WIDTH),
            pl.BlockSpec((KEY_SUB, KEY_SUB), lambda bi: (0, 0)),
        ],
        out_specs=whole(t, SB_WIDTH),
        out_shape=jax.ShapeDtypeStruct((b, t, SB_WIDTH), jnp.float32),
        compiler_params=pltpu.CompilerParams(
            dimension_semantics=("arbitrary",), vmem_limit_bytes=V7X_VMEM_LIMIT_BYTES),
        name="sb_attn_short",
    )(qbd, k_cache_t, v_cache_t, k_new, v_new, u_neg)


def _mix_ffn_kernel(*refs, seq_tiles, has_hist):
    if has_hist:
        (x_ref, an_ref, ob_ref, h1_ref, h2_ref, g_b_ref, w_out_ref, g_pm_ref, g_pf_ref, w_up_ref,
         cw_ref, cb_ref, w_dn_ref, g_po_ref, y_ref, up_ref, buf_ref, x1_buf, hn_buf) = refs
    else:
        (x_ref, an_ref, ob_ref, g_b_ref, w_out_ref, g_pm_ref, g_pf_ref, w_up_ref,
         cw_ref, cb_ref, w_dn_ref, g_po_ref, y_ref, hist_ref, buf_ref, x1_buf, hn_buf) = refs
    tm = x_ref.shape[0]
    pad = V7X_SUBLANES
    step = pl.program_id(0)
    slot = step % 2
    prev = 1 - slot

    @pl.when(step == 0)
    def _():
        x1_buf[1] = jnp.zeros(x1_buf.shape[1:], x1_buf.dtype)
        hn_buf[1] = jnp.zeros(hn_buf.shape[1:], hn_buf.dtype)

    x1 = x1_buf[prev]
    h2 = hn_buf[prev]

    def mixer_epilogue():
        bn = _rms(ob_ref[...], g_b_ref[...]).astype(jnp.bfloat16)
        mix = _dot(an_ref[...], w_out_ref[0:SGU_WIDTH, :]) + _dot(bn, w_out_ref[SGU_WIDTH:, :])
        x1_new = x_ref[...] + _rms(mix, g_pm_ref[...])
        x1_buf[slot] = x1_new
        hn_buf[slot] = _rms(x1_new, g_pf_ref[...]).astype(jnp.bfloat16)

    taps = range(CONV_WIDTH)

    def head_rows(s):
        return buf_ref.at[s, pad:pad + s, :]

    def zero_head_rows():
        for s in taps[1:]:
            head_rows(s)[...] = jnp.zeros((s, buf_ref.shape[2]), jnp.float32)

    if has_hist:
        t_in_seq = lax.broadcasted_iota(jnp.int32, (tm, FF_CHUNK), 0) % seq_tiles
        zero_head_rows()
    else:
        @pl.when(step == 0)
        def _():
            for s in taps[1:]:
                buf_ref[s, pad + tm:pad + tm + s, :] = jnp.zeros((s, buf_ref.shape[2]), jnp.float32)

        for s in taps[1:]:
            head_rows(s)[...] = buf_ref[s, pad + tm:pad + tm + s, :]
        pl.when((step - 1) % seq_tiles == 0)(zero_head_rows)

    def up_dot(col):
        up = _dot(h2, w_up_ref[:, col:col + FF_CHUNK])
        for s in taps:
            buf_ref[s, pad + s:pad + s + tm, col:col + FF_CHUNK] = up

    def conv(col, scale):
        up, m1, m2 = (buf_ref[s, pad:pad + tm, col:col + FF_CHUNK] for s in taps)
        if has_hist:
            up_ref[:, col:col + FF_CHUNK] = up
            m1 = jnp.where(t_in_seq >= 1, m1, 0.0) + h1_ref[:, col:col + FF_CHUNK]
            m2 = jnp.where(t_in_seq >= 2, m2, 0.0) + h2_ref[:, col:col + FF_CHUNK]
        else:
            hist_ref[0, :, col:col + FF_CHUNK] = up[tm - 2:tm, :]
        w = cw_ref[:, col:col + FF_CHUNK] * scale
        b = cb_ref[:, col:col + FF_CHUNK] * scale
        return m2 * w[0:1] + m1 * w[1:2] + up * w[2:3] + b

    n_chunks = D_FF // FF_CHUNK
    up_dot(0)
    up_dot(D_FF)
    mixer_epilogue()
    f = jnp.zeros((tm, D_MODEL), jnp.float32)
    prev_act = None
    for c in range(n_chunks):
        col = c * FF_CHUNK
        if c + 1 < n_chunks:
            up_dot(col + FF_CHUNK)
            up_dot(D_FF + col + FF_CHUNK)
        if prev_act is not None:
            f = f + _dot(prev_act, w_dn_ref[col - FF_CHUNK:col, :])
        gate = conv(col, 1.0)
        half_val = conv(D_FF + col, 0.5)
        inner = gate * (GELU_C1 + GELU_C2 * (gate * gate))
        prev_act = ((gate * half_val) * (1.0 + jnp.tanh(inner))).astype(jnp.bfloat16)
    f = f + _dot(prev_act, w_dn_ref[D_FF - FF_CHUNK:D_FF, :])
    y_ref[...] = x1 + _rms(f, g_po_ref[...])


def _mix_ffn(x, an, ob, hist_taps, g_b, w_out_b, g_pm, g_pf, w_up_b, conv_w, conv_b, w_dn_b, g_po,
             *, tm, seq_len):
    m = x.shape[0]
    n_tiles = m // tm
    row = lambda i: (jnp.minimum(i, n_tiles - 1), 0)
    done = lambda i: (jnp.maximum(i - 1, 0), 0)
    fixed = lambda i: (0, 0)
    has_hist = hist_taps is not None
    once = dict(pipeline_mode=pl.Buffered(1))
    in_specs = [
        pl.BlockSpec((tm, D_MODEL), row),
        pl.BlockSpec((tm, SGU_WIDTH), row),
        pl.BlockSpec((tm, SB_WIDTH), row),
    ]
    args = [x, an, ob]
    if has_hist:
        assert tm % seq_len == 0
        seq_tiles = seq_len
        in_specs += [pl.BlockSpec((tm, 2 * D_FF), done)] * 2
        args += list(hist_taps)
        out_specs = [pl.BlockSpec((tm, D_MODEL), done), pl.BlockSpec((tm, 2 * D_FF), done)]
        out_shape = [jax.ShapeDtypeStruct((m, D_MODEL), jnp.float32),
                     jax.ShapeDtypeStruct((m, 2 * D_FF), jnp.float32)]
    else:
        assert seq_len % tm == 0
        seq_tiles = seq_len // tm
        out_specs = [pl.BlockSpec((tm, D_MODEL), done),
                     pl.BlockSpec((1, CONV_WIDTH - 1, 2 * D_FF),
                                  lambda i: (jnp.maximum(i - 1, 0) // seq_tiles, 0, 0))]
        out_shape = [jax.ShapeDtypeStruct((m, D_MODEL), jnp.float32),
                     jax.ShapeDtypeStruct((m // seq_len, CONV_WIDTH - 1, 2 * D_FF), jnp.float32)]
    in_specs += [
        pl.BlockSpec((1, SB_WIDTH), fixed),
        pl.BlockSpec((D_MODEL, D_MODEL), fixed, **once),
        pl.BlockSpec((1, D_MODEL), fixed),
        pl.BlockSpec((1, D_MODEL), fixed),
        pl.BlockSpec((D_MODEL, 2 * D_FF), fixed, **once),
        pl.BlockSpec((CONV_WIDTH, 2 * D_FF), fixed),
        pl.BlockSpec((1, 2 * D_FF), fixed),
        pl.BlockSpec((D_FF, D_MODEL), fixed, **once),
        pl.BlockSpec((1, D_MODEL), fixed),
    ]
    args += [g_b, w_out_b, g_pm, g_pf, w_up_b, conv_w, conv_b, w_dn_b, g_po]
    return pl.pallas_call(
        functools.partial(_mix_ffn_kernel, seq_tiles=seq_tiles, has_hist=has_hist),
        grid=(n_tiles + 1,),
        in_specs=in_specs,
        out_specs=out_specs,
        out_shape=out_shape,
        scratch_shapes=[pltpu.VMEM((CONV_WIDTH, tm + 2 * V7X_SUBLANES, 2 * D_FF), jnp.float32),
                        pltpu.VMEM((2, tm, D_MODEL), jnp.float32),
                        pltpu.VMEM((2, tm, D_MODEL), jnp.bfloat16)],
        compiler_params=pltpu.CompilerParams(
            dimension_semantics=("arbitrary",), vmem_limit_bytes=V7X_VMEM_LIMIT_BYTES),
        name="mix_ffn",
    )(*args)


def _suffix_matrix():
    r = lax.broadcasted_iota(jnp.int32, (KEY_SUB, KEY_SUB), 0)
    c = lax.broadcasted_iota(jnp.int32, (KEY_SUB, KEY_SUB), 1)
    return jnp.where(r > c, -1.0, 0.0).astype(jnp.bfloat16)


def _layer(x, past_k, past_v, conv_hist, p, *, tm_proj, tq, tm_ffn):
    bsz, t, _ = x.shape
    m = bsz * t
    row = lambda a: a.reshape(1, -1)
    bf16 = jnp.bfloat16

    seq = min(t, SGU_CHUNK)
    w_s = jnp.where(jnp.tril(jnp.ones((seq, seq), bool))[None], p["w_spatial"][:, :seq, :seq], 0.0)
    reps = SGU_CHUNK // seq
    wsp = jnp.einsum("ab,hts->hatbs", jnp.eye(reps, dtype=w_s.dtype), w_s)
    wsp = wsp.reshape(SGU_HEADS, SGU_CHUNK, SGU_CHUNK).astype(bf16)
    b_s = jnp.tile(p["b_spatial"][:, :seq].T, (reps, 1))
    bsp = jnp.repeat(b_s, SGU_HEAD_DIM, axis=1)

    va, an, q, kb, vb, k, v = _in_proj(
        x.reshape(m, D_MODEL), row(p["g_pre_mix"]), p["w_in"].astype(bf16), row(p["ln_v_g"]),
        row(p["ln_v_b"]), wsp, bsp, row(p["g_out_a"]), tm=tm_proj, seq_len=t)

    past = past_k.shape[1]
    q3 = q.reshape(bsz, t, SB_WIDTH)
    k3 = kb.reshape(bsz, t, SB_WIDTH)
    v3 = vb.reshape(bsz, t, SB_WIDTH)
    if SB_HEADS * t == V7X_LANES and past % KEY_SUB == 0:
        by_key = lambda c: jnp.transpose(c, (0, 2, 3, 1)).reshape(bsz, SB_WIDTH, past)
        ob = _sb_attn_short(q3, by_key(past_k), by_key(past_v), k3, v3, _suffix_matrix())
    else:
        t_keys = -(-(past + t) // KEY_TILE) * KEY_TILE
        fill = jnp.zeros((bsz, t_keys - past - t, SB_WIDTH), bf16)
        cat = lambda old, new: jnp.concatenate(
            [old.reshape(bsz, past, SB_WIDTH).astype(bf16), new, fill], axis=1)
        ob = _sb_attn(q3, cat(past_k, k3), cat(past_v, v3), _suffix_matrix(), past=past, tq=tq)

    if conv_hist is None:
        hist_taps = None
    else:
        z = jnp.zeros((bsz, t - 1, 2 * D_FF), jnp.float32)
        tap1 = jnp.concatenate([conv_hist[:, 1:], z], axis=1)
        tap2 = jnp.concatenate([conv_hist, z[:, 1:]], axis=1)
        hist_taps = (tap1.reshape(m, 2 * D_FF), tap2.reshape(m, 2 * D_FF))
    y, extra = _mix_ffn(
        x.reshape(m, D_MODEL), an, ob.reshape(m, SB_WIDTH), hist_taps, row(p["g_out_b"]),
        p["w_out"].astype(bf16), row(p["g_post_mix"]), row(p["g_pre_ffn"]), p["w_up"].astype(bf16),
        p["conv_w"], row(p["conv_b"]), p["w_down"].astype(bf16), row(p["g_post_ffn"]),
        tm=tm_ffn, seq_len=t)
    if conv_hist is None:
        new_hist = extra
    else:
        new_hist = extra.reshape(bsz, t, 2 * D_FF)[:, t - (CONV_WIDTH - 1):]
    if k.ndim == 3:
        heads_last = lambda a: jnp.transpose(a.reshape(bsz, SB_HEADS, SB_HEAD_DIM, t), (0, 3, 1, 2))
    else:
        heads_last = lambda a: a.reshape(bsz, t, SB_HEADS, SB_HEAD_DIM)
    return (y.reshape(bsz, t, D_MODEL), heads_last(k), heads_last(v),
            va.reshape(bsz, t, SGU_WIDTH), new_hist)


def kernel(x_prompt, x_sample, cache_sb_k, cache_sb_v, cache_ffn_conv, w_in, g_pre_mix, ln_v_g, ln_v_b, w_spatial, b_spatial, g_out_a, g_out_b, w_out, g_post_mix, g_pre_ffn, w_up, conv_w, conv_b, w_down, g_post_ffn):
    names = ("w_in", "g_pre_mix", "ln_v_g", "ln_v_b", "w_spatial", "b_spatial", "g_out_a", "g_out_b",
             "w_out", "g_post_mix", "g_pre_ffn", "w_up", "conv_w", "conv_b", "w_down", "g_post_ffn")
    stacked = (w_in, g_pre_mix, ln_v_g, ln_v_b, w_spatial, b_spatial, g_out_a, g_out_b, w_out,
               g_post_mix, g_pre_ffn, w_up, conv_w, conv_b, w_down, g_post_ffn)
    depth = w_in.shape[0]
    yp, ys = x_prompt, x_sample
    bp = x_prompt.shape[0]
    empty_kv = jnp.zeros((bp, 0, SB_HEADS, SB_HEAD_DIM), x_prompt.dtype)
    outs = [[] for _ in range(7)]
    for l in range(depth):
        p = {n: a[l] for n, a in zip(names, stacked)}
        yp, kp, vp, _, cp = _layer(yp, empty_kv, empty_kv, None, p, tm_proj=512, tq=1024, tm_ffn=256)
        ys, ksm, vsm, vas, cs = _layer(ys, cache_sb_k[l], cache_sb_v[l], cache_ffn_conv[l], p,
                                       tm_proj=512, tq=ys.shape[1], tm_ffn=128)
        for lst, a in zip(outs, (kp, vp, ksm, vsm, vas, cp, cs)):
            lst.append(a)
    return (yp, ys) + tuple(jnp.stack(lst) for lst in outs)
```

```python
import functools
import math

import jax
import jax.numpy as jnp
from jax import lax
from jax.experimental import pallas as pl
from jax.experimental.pallas import tpu as pltpu

D_MODEL = 1024
SGU_WIDTH = 512
SGU_HEADS = 4
SGU_HEAD_DIM = 128
SGU_CHUNK = 128
SB_WIDTH = 512
SB_HEADS = 8
SB_HEAD_DIM = 64
D_FF = 2816
CONV_WIDTH = 3
EPS = 1e-6

V7X_LANES = 128
V7X_SUBLANES = 8
V7X_MXU_DIM = 256
V7X_VMEM_LIMIT_BYTES = 56 * 1024 * 1024

HEADS_PER_BLOCK = V7X_LANES // SB_HEAD_DIM
KEY_SUB = V7X_MXU_DIM
KEY_SUBS_PER_TILE = 4
KEY_TILE = KEY_SUB * KEY_SUBS_PER_TILE
FF_CHUNK = V7X_MXU_DIM
SKEW = 2
UNREACHABLE = -1e30
GELU_C1 = math.sqrt(2.0 / math.pi)
GELU_C2 = 0.044715 * GELU_C1


def _rms(x, g):
    return x * lax.rsqrt(jnp.mean(x * x, axis=-1, keepdims=True) + EPS) * g


def _gelu(x):
    return jax.nn.gelu(x)


def _dot(a, b):
    return jnp.dot(a, b, preferred_element_type=jnp.float32)


def _in_proj_kernel(x_ref, g_pre_ref, w_in_ref, ln_g_ref, ln_b_ref, wsp_ref, bsp_ref, g_a_ref,
                    va_ref, an_ref, q_ref, kb_ref, vb_ref, k_ref, v_ref, outa_ref, *, q_scale):
    tm = x_ref.shape[0]
    h = _rms(x_ref[...], g_pre_ref[...]).astype(jnp.bfloat16)

    def proj(col):
        return _dot(h, w_in_ref[:, col:col + SGU_WIDTH])

    o = 2 * SGU_WIDTH
    pu = proj(0)
    pg = proj(SGU_WIDTH)
    q_ref[...] = (proj(o) * q_scale).astype(jnp.bfloat16)
    k = proj(o + SB_WIDTH)
    kb_ref[...] = k.astype(jnp.bfloat16)
    v = proj(o + 2 * SB_WIDTH)
    vb_ref[...] = v.astype(jnp.bfloat16)
    if len(k_ref.shape) == 3:
        k_ref[0] = k.T
        v_ref[0] = v.T
    else:
        k_ref[...] = k
        v_ref[...] = v

    u = _gelu(pu)
    gv = _gelu(pg)
    mu = jnp.mean(gv, axis=-1, keepdims=True)
    gc = gv - mu
    va = gc * lax.rsqrt(jnp.mean(gc * gc, axis=-1, keepdims=True) + EPS) * ln_g_ref[...] + ln_b_ref[...]
    va_ref[...] = va
    va_b = va.astype(jnp.bfloat16)
    for c in range(tm // SGU_CHUNK):
        r0 = c * SGU_CHUNK
        for hd in range(SGU_HEADS):
            c0 = hd * SGU_HEAD_DIM
            mixed = _dot(wsp_ref[hd], va_b[r0:r0 + SGU_CHUNK, c0:c0 + SGU_HEAD_DIM])
            mixed = mixed + bsp_ref[:, c0:c0 + SGU_HEAD_DIM]
            outa_ref[r0:r0 + SGU_CHUNK, c0:c0 + SGU_HEAD_DIM] = u[r0:r0 + SGU_CHUNK, c0:c0 + SGU_HEAD_DIM] * mixed
    an_ref[...] = _rms(outa_ref[...], g_a_ref[...]).astype(jnp.bfloat16)


def _in_proj(x, g_pre, w_in_b, ln_g, ln_b, wsp, bsp, g_a, *, tm, seq_len):
    m = x.shape[0]
    in_width = w_in_b.shape[1]
    row = lambda i: (i, 0)
    fixed2 = lambda i: (0, 0)
    fixed3 = lambda i: (0, 0, 0)
    f32 = jnp.float32
    bf16 = jnp.bfloat16
    wide = lambda dt: jax.ShapeDtypeStruct((m, SGU_WIDTH), dt)
    blk = pl.BlockSpec((tm, SGU_WIDTH), row)
    if seq_len % tm == 0 and tm % V7X_LANES == 0:
        tiles = seq_len // tm
        kv_shape = jax.ShapeDtypeStruct((m // seq_len, SB_WIDTH, seq_len), f32)
        kv_blk = pl.BlockSpec((1, SB_WIDTH, tm), lambda i: (i // tiles, 0, i % tiles))
    else:
        kv_shape, kv_blk = wide(f32), blk
    return pl.pallas_call(
        functools.partial(_in_proj_kernel, q_scale=SB_HEAD_DIM ** -0.5),
        grid=(m // tm,),
        in_specs=[
            pl.BlockSpec((tm, D_MODEL), row),
            pl.BlockSpec((1, D_MODEL), fixed2),
            pl.BlockSpec((D_MODEL, in_width), fixed2),
            pl.BlockSpec((1, SGU_WIDTH), fixed2),
            pl.BlockSpec((1, SGU_WIDTH), fixed2),
            pl.BlockSpec((SGU_HEADS, SGU_CHUNK, SGU_CHUNK), fixed3),
            pl.BlockSpec((SGU_CHUNK, SGU_WIDTH), fixed2),
            pl.BlockSpec((1, SGU_WIDTH), fixed2),
        ],
        out_specs=[blk] * 5 + [kv_blk] * 2,
        out_shape=[wide(f32), wide(bf16), wide(bf16), wide(bf16), wide(bf16), kv_shape, kv_shape],
        scratch_shapes=[pltpu.VMEM((tm, SGU_WIDTH), f32)],
        compiler_params=pltpu.CompilerParams(
            dimension_semantics=("arbitrary",), vmem_limit_bytes=V7X_VMEM_LIMIT_BYTES),
        name="in_proj",
    )(x, g_pre, w_in_b, ln_g, ln_b, wsp, bsp, g_a)


def _sb_attn_kernel(q_ref, k_ref, v_ref, u_ref, o_ref, acc_ref, carry_ref, *, past, tq):
    i = pl.program_id(2)
    lo = past + i * tq
    hi = lo + tq - 1
    n_full = lo // KEY_TILE
    n_tot = jnp.maximum(hi - 1, 0) // KEY_TILE + 1

    lane = lax.broadcasted_iota(jnp.int32, (tq, V7X_LANES), 1)
    q2 = q_ref[0]
    zero = jnp.zeros_like(q2)
    q_heads = [jnp.where((lane // SB_HEAD_DIM) == hd, q2, zero) for hd in range(HEADS_PER_BLOCK)]
    u_neg = u_ref[...]

    acc_ref[...] = jnp.zeros_like(acc_ref)
    carry_ref[...] = jnp.zeros_like(carry_ref)

    diagonal = tq == KEY_TILE and past % KEY_TILE == 0
    below_diagonal = (lax.broadcasted_iota(jnp.int32, (KEY_SUB, KEY_SUB), 1)
                      < lax.broadcasted_iota(jnp.int32, (KEY_SUB, KEY_SUB), 0))

    def tile(j, masked):
        chains = [(sb, hd) for sb in reversed(range(KEY_SUBS_PER_TILE)) for hd in range(HEADS_PER_BLOCK)]

        def span(sb):
            k0 = pl.multiple_of(j * KEY_TILE + sb * KEY_SUB, KEY_SUB)
            r0 = sb * KEY_SUB if (masked and diagonal) else 0
            return k0, r0

        def scores(sb, hd):
            k0, r0 = span(sb)
            k2 = k_ref[0, pl.ds(k0, KEY_SUB), :]
            z = lax.dot_general(q_heads[hd][r0:], k2, (((1,), (1,)), ((), ())),
                                preferred_element_type=jnp.float32)
            if masked and diagonal:
                top = jnp.where(below_diagonal, z[:KEY_SUB], UNREACHABLE)
                z = top if tq - r0 == KEY_SUB else jnp.concatenate([top, z[KEY_SUB:]], axis=0)
            elif masked:
                q_pos = lo + lax.broadcasted_iota(jnp.int32, (tq, KEY_SUB), 0)
                k_pos = k0 + lax.broadcasted_iota(jnp.int32, (tq, KEY_SUB), 1)
                z = jnp.where(k_pos < q_pos, z, UNREACHABLE)
            return z

        def suffix(z):
            zb = z.astype(jnp.bfloat16)
            z_pos = jnp.maximum(zb, 0.0)
            z_neg = jnp.minimum(zb, 0.0)
            l = jnp.log(1.0 + jnp.exp(z_neg - z_pos))
            sp = z_pos + l
            excl = _dot(sp, u_neg)
            return z_neg - l, excl, excl[:, 0:1] - sp[:, 0:1].astype(jnp.float32)

        def weigh(sb, hd, own, excl, total):
            k0, r0 = span(sb)
            v2 = v_ref[0, pl.ds(k0, KEY_SUB), :]
            carry = carry_ref[hd, r0:]
            w = jnp.exp((excl + carry).astype(jnp.bfloat16) + own)
            acc_ref[hd, r0:] += _dot(w, v2)
            carry_ref[hd, r0:] = carry + total

        n = len(chains)
        zs, sums = {}, {}
        for step in range(n + SKEW):
            if step < n:
                zs[step] = scores(*chains[step])
            if 0 <= step - 1 < n:
                sums[step - 1] = suffix(zs.pop(step - 1))
            if 0 <= step - SKEW < n:
                weigh(*chains[step - SKEW], *sums.pop(step - SKEW))

    def masked_body(t, c):
        tile(n_tot - 1 - t, True)
        return c

    def full_body(t, c):
        tile(n_full - 1 - t, False)
        return c

    lax.fori_loop(0, n_tot - n_full, masked_body, 0)
    lax.fori_loop(0, n_full, full_body, 0)

    out = acc_ref[0]
    for hd in range(1, HEADS_PER_BLOCK):
        out = jnp.where((lane // SB_HEAD_DIM) == hd, acc_ref[hd], out)
    o_ref[0] = out


def _sb_attn(q, k_all, v_all, u_neg, *, past, tq):
    b, t, _ = q.shape
    tk = k_all.shape[1]
    n_blk = SB_WIDTH // V7X_LANES
    return pl.pallas_call(
        functools.partial(_sb_attn_kernel, past=past, tq=tq),
        grid=(b, n_blk, t // tq),
        in_specs=[
            pl.BlockSpec((1, tq, V7X_LANES), lambda bi, hp, i: (bi, i, hp)),
            pl.BlockSpec((1, tk, V7X_LANES), lambda bi, hp, i: (bi, 0, hp)),
            pl.BlockSpec((1, tk, V7X_LANES), lambda bi, hp, i: (bi, 0, hp)),
            pl.BlockSpec((KEY_SUB, KEY_SUB), lambda bi, hp, i: (0, 0)),
        ],
        out_specs=pl.BlockSpec((1, tq, V7X_LANES), lambda bi, hp, i: (bi, i, hp)),
        out_shape=jax.ShapeDtypeStruct((b, t, SB_WIDTH), jnp.float32),
        scratch_shapes=[
            pltpu.VMEM((HEADS_PER_BLOCK, tq, V7X_LANES), jnp.float32),
            pltpu.VMEM((HEADS_PER_BLOCK, tq, 1), jnp.float32),
        ],
        compiler_params=pltpu.CompilerParams(
            dimension_semantics=("arbitrary", "arbitrary", "arbitrary"),
            vmem_limit_bytes=V7X_VMEM_LIMIT_BYTES),
        name="sb_attn",
    )(q, k_all, v_all, u_neg)


def _sb_attn_short_kernel(qbd_ref, kc_ref, vc_ref, kn_ref, vn_ref, u_ref, o_ref, *, t):
    past = kc_ref.shape[2]
    n_cache = past // KEY_SUB
    qbd = qbd_ref[0]
    u_neg = u_ref[...]
    fill = jnp.zeros((KEY_SUB - t, SB_WIDTH), jnp.bfloat16)
    nt = (((1,), (1,)), ((), ()))

    def cached(ref, j):
        return ref[0, :, j * KEY_SUB:(j + 1) * KEY_SUB].astype(jnp.bfloat16)

    def fresh(ref):
        return jnp.concatenate([ref[0], fill], axis=0)

    zs = [_dot(qbd, cached(kc_ref, j)) for j in range(n_cache)]
    z_new = lax.dot_general(qbd, fresh(kn_ref), nt, preferred_element_type=jnp.float32)
    key = lax.broadcasted_iota(jnp.int32, (V7X_LANES, KEY_SUB), 1)
    tt = lax.broadcasted_iota(jnp.int32, (V7X_LANES, KEY_SUB), 0) % t
    zs.append(jnp.where(key < tt, z_new, UNREACHABLE))
    expo, tot = [], []
    for z in zs:
        zb = z.astype(jnp.bfloat16)
        l = jnp.log(1.0 + jnp.exp(-jnp.abs(zb)))
        sp = jnp.maximum(zb, 0.0) + l
        own = (jnp.minimum(zb, 0.0) - l).astype(jnp.float32)
        excl = _dot(sp, u_neg)
        expo.append(own + excl)
        tot.append(excl[:, 0:1] - sp[:, 0:1].astype(jnp.float32))
    carry = jnp.zeros((V7X_LANES, 1), jnp.float32)
    acc = jnp.zeros((V7X_LANES, SB_WIDTH), jnp.float32)
    for j in reversed(range(n_cache + 1)):
        w = jnp.exp(expo[j] + carry).astype(jnp.bfloat16)
        if j == n_cache:
            acc = acc + _dot(w, fresh(vn_ref))
        else:
            acc = acc + lax.dot_general(w, cached(vc_ref, j), nt, preferred_element_type=jnp.float32)
        carry = carry + tot[j]
    col_head = lax.broadcasted_iota(jnp.int32, (t, SB_WIDTH), 1) // SB_HEAD_DIM
    out = jnp.zeros((t, SB_WIDTH), jnp.float32)
    for hd in range(SB_HEADS):
        out = jnp.where(col_head == hd, acc[hd * t:(hd + 1) * t, :], out)
    o_ref[0] = out


def _sb_attn_short(q, k_cache_t, v_cache_t, k_new, v_new, u_neg):
    b, t, _ = q.shape
    past = k_cache_t.shape[2]
    assert SB_HEADS * t == V7X_LANES and past % KEY_SUB == 0 and t <= KEY_SUB
    q4 = q.reshape(b, t, SB_HEADS, SB_HEAD_DIM)
    qbd = jnp.einsum("bthd,hg->bhtgd", q4, jnp.eye(SB_HEADS, dtype=q.dtype)).reshape(b, V7X_LANES, SB_WIDTH)
    whole = lambda *shape: pl.BlockSpec((1,) + shape, lambda bi: (bi, 0, 0))
    return pl.pallas_call(
        functools.partial(_sb_attn_short_kernel, t=t),
        grid=(b,),
        in_specs=[
            whole(V7X_LANES, SB_WIDTH),
            whole(SB_WIDTH, past), whole(SB_WIDTH, past), whole(t, SB_WIDTH), whole(t, SB_WIDTH),
            pl.BlockSpec((KEY_SUB, KEY_SUB), lambda bi: (0, 0)),
        ],
        out_specs=whole(t, SB_WIDTH),
        out_shape=jax.ShapeDtypeStruct((b, t, SB_WIDTH), jnp.float32),
        compiler_params=pltpu.CompilerParams(
            dimension_semantics=("arbitrary",), vmem_limit_bytes=V7X_VMEM_LIMIT_BYTES),
        name="sb_attn_short",
    )(qbd, k_cache_t, v_cache_t, k_new, v_new, u_neg)


def _mix_ffn_kernel(*refs, seq_tiles, has_hist):
    if has_hist:
        (x_ref, an_ref, ob_ref, h1_ref, h2_ref, g_b_ref, w_out_ref, g_pm_ref, g_pf_ref, w_up_ref,
         cw_ref, cb_ref, w_dn_ref, g_po_ref, y_ref, up_ref, buf_ref, x1_buf, hn_buf) = refs
    else:
        (x_ref, an_ref, ob_ref, g_b_ref, w_out_ref, g_pm_ref, g_pf_ref, w_up_ref,
         cw_ref, cb_ref, w_dn_ref, g_po_ref, y_ref, hist_ref, buf_ref, x1_buf, hn_buf) = refs
    tm = x_ref.shape[0]
    pad = V7X_SUBLANES
    step = pl.program_id(0)
    slot = step % 2
    prev = 1 - slot

    @pl.when(step == 0)
    def _():
        x1_buf[1] = jnp.zeros(x1_buf.shape[1:], x1_buf.dtype)
        hn_buf[1] = jnp.zeros(hn_buf.shape[1:], hn_buf.dtype)

    x1 = x1_buf[prev]
    h2 = hn_buf[prev]

    def mixer_epilogue():
        bn = _rms(ob_ref[...], g_b_ref[...]).astype(jnp.bfloat16)
        mix = _dot(an_ref[...], w_out_ref[0:SGU_WIDTH, :]) + _dot(bn, w_out_ref[SGU_WIDTH:, :])
        x1_new = x_ref[...] + _rms(mix, g_pm_ref[...])
        x1_buf[slot] = x1_new
        hn_buf[slot] = _rms(x1_new, g_pf_ref[...]).astype(jnp.bfloat16)

    taps = range(CONV_WIDTH)

    def head_rows(s):
        return buf_ref.at[s, pad:pad + s, :]

    def zero_head_rows():
        for s in taps[1:]:
            head_rows(s)[...] = jnp.zeros((s, buf_ref.shape[2]), jnp.float32)

    if has_hist:
        t_in_seq = lax.broadcasted_iota(jnp.int32, (tm, FF_CHUNK), 0) % seq_tiles
        zero_head_rows()
    else:
        @pl.when(step == 0)
        def _():
            for s in taps[1:]:
                buf_ref[s, pad + tm:pad + tm + s, :] = jnp.zeros((s, buf_ref.shape[2]), jnp.float32)

        for s in taps[1:]:
            head_rows(s)[...] = buf_ref[s, pad + tm:pad + tm + s, :]
        pl.when((step - 1) % seq_tiles == 0)(zero_head_rows)

    def up_dot(col):
        up = _dot(h2, w_up_ref[:, col:col + FF_CHUNK])
        for s in taps:
            buf_ref[s, pad + s:pad + s + tm, col:col + FF_CHUNK] = up

    def conv(col, scale):
        up, m1, m2 = (buf_ref[s, pad:pad + tm, col:col + FF_CHUNK] for s in taps)
        if has_hist:
            up_ref[:, col:col + FF_CHUNK] = up
            m1 = jnp.where(t_in_seq >= 1, m1, 0.0) + h1_ref[:, col:col + FF_CHUNK]
            m2 = jnp.where(t_in_seq >= 2, m2, 0.0) + h2_ref[:, col:col + FF_CHUNK]
        else:
            hist_ref[0, :, col:col + FF_CHUNK] = up[tm - 2:tm, :]
        w = cw_ref[:, col:col + FF_CHUNK] * scale
        b = cb_ref[:, col:col + FF_CHUNK] * scale
        return m2 * w[0:1] + m1 * w[1:2] + up * w[2:3] + b

    n_chunks = D_FF // FF_CHUNK
    up_dot(0)
    up_dot(D_FF)
    mixer_epilogue()
    f = jnp.zeros((tm, D_MODEL), jnp.float32)
    prev_act = None
    for c in range(n_chunks):
        col = c * FF_CHUNK
        if c + 1 < n_chunks:
            up_dot(col + FF_CHUNK)
            up_dot(D_FF + col + FF_CHUNK)
        if prev_act is not None:
            f = f + _dot(prev_act, w_dn_ref[col - FF_CHUNK:col, :])
        gate = conv(col, 1.0)
        half_val = conv(D_FF + col, 0.5)
        inner = gate * (GELU_C1 + GELU_C2 * (gate * gate))
        prev_act = ((gate * half_val) * (1.0 + jnp.tanh(inner))).astype(jnp.bfloat16)
    f = f + _dot(prev_act, w_dn_ref[D_FF - FF_CHUNK:D_FF, :])
    y_ref[...] = x1 + _rms(f, g_po_ref[...])


def _mix_ffn(x, an, ob, hist_taps, g_b, w_out_b, g_pm, g_pf, w_up_b, conv_w, conv_b, w_dn_b, g_po,
             *, tm, seq_len):
    m = x.shape[0]
    n_tiles = m // tm
    row = lambda i: (jnp.minimum(i, n_tiles - 1), 0)
    done = lambda i: (jnp.maximum(i - 1, 0), 0)
    fixed = lambda i: (0, 0)
    has_hist = hist_taps is not None
    once = dict(pipeline_mode=pl.Buffered(1))
    in_specs = [
        pl.BlockSpec((tm, D_MODEL), row),
        pl.BlockSpec((tm, SGU_WIDTH), row),
        pl.BlockSpec((tm, SB_WIDTH), row),
    ]
    args = [x, an, ob]
    if has_hist:
        assert tm % seq_len == 0
        seq_tiles = seq_len
        in_specs += [pl.BlockSpec((tm, 2 * D_FF), done)] * 2
        args += list(hist_taps)
        out_specs = [pl.BlockSpec((tm, D_MODEL), done), pl.BlockSpec((tm, 2 * D_FF), done)]
        out_shape = [jax.ShapeDtypeStruct((m, D_MODEL), jnp.float32),
                     jax.ShapeDtypeStruct((m, 2 * D_FF), jnp.float32)]
    else:
        assert seq_len % tm == 0
        seq_tiles = seq_len // tm
        out_specs = [pl.BlockSpec((tm, D_MODEL), done),
                     pl.BlockSpec((1, CONV_WIDTH - 1, 2 * D_FF),
                                  lambda i: (jnp.maximum(i - 1, 0) // seq_tiles, 0, 0))]
        out_shape = [jax.ShapeDtypeStruct((m, D_MODEL), jnp.float32),
                     jax.ShapeDtypeStruct((m // seq_len, CONV_WIDTH - 1, 2 * D_FF), jnp.float32)]
    in_specs += [
        pl.BlockSpec((1, SB_WIDTH), fixed),
        pl.BlockSpec((D_MODEL, D_MODEL), fixed, **once),
        pl.BlockSpec((1, D_MODEL), fixed),
        pl.BlockSpec((1, D_MODEL), fixed),
        pl.BlockSpec((D_MODEL, 2 * D_FF), fixed, **once),
        pl.BlockSpec((CONV_WIDTH, 2 * D_FF), fixed),
        pl.BlockSpec((1, 2 * D_FF), fixed),
        pl.BlockSpec((D_FF, D_MODEL), fixed, **once),
        pl.BlockSpec((1, D_MODEL), fixed),
    ]
    args += [g_b, w_out_b, g_pm, g_pf, w_up_b, conv_w, conv_b, w_dn_b, g_po]
    return pl.pallas_call(
        functools.partial(_mix_ffn_kernel, seq_tiles=seq_tiles, has_hist=has_hist),
        grid=(n_tiles + 1,),
        in_specs=in_specs,
        out_specs=out_specs,
        out_shape=out_shape,
        scratch_shapes=[pltpu.VMEM((CONV_WIDTH, tm + 2 * V7X_SUBLANES, 2 * D_FF), jnp.float32),
                        pltpu.VMEM((2, tm, D_MODEL), jnp.float32),
                        pltpu.VMEM((2, tm, D_MODEL), jnp.bfloat16)],
        compiler_params=pltpu.CompilerParams(
            dimension_semantics=("arbitrary",), vmem_limit_bytes=V7X_VMEM_LIMIT_BYTES),
        name="mix_ffn",
    )(*args)


def _suffix_matrix():
    r = lax.broadcasted_iota(jnp.int32, (KEY_SUB, KEY_SUB), 0)
    c = lax.broadcasted_iota(jnp.int32, (KEY_SUB, KEY_SUB), 1)
    return jnp.where(r > c, -1.0, 0.0).astype(jnp.bfloat16)


def _layer(x, past_k, past_v, conv_hist, p, *, tm_proj, tq, tm_ffn):
    bsz, t, _ = x.shape
    m = bsz * t
    row = lambda a: a.reshape(1, -1)
    bf16 = jnp.bfloat16

    seq = min(t, SGU_CHUNK)
    w_s = jnp.where(jnp.tril(jnp.ones((seq, seq), bool))[None], p["w_spatial"][:, :seq, :seq], 0.0)
    reps = SGU_CHUNK // seq
    wsp = jnp.einsum("ab,hts->hatbs", jnp.eye(reps, dtype=w_s.dtype), w_s)
    wsp = wsp.reshape(SGU_HEADS, SGU_CHUNK, SGU_CHUNK).astype(bf16)
    b_s = jnp.tile(p["b_spatial"][:, :seq].T, (reps, 1))
    bsp = jnp.repeat(b_s, SGU_HEAD_DIM, axis=1)

    va, an, q, kb, vb, k, v = _in_proj(
        x.reshape(m, D_MODEL), row(p["g_pre_mix"]), p["w_in"].astype(bf16), row(p["ln_v_g"]),
        row(p["ln_v_b"]), wsp, bsp, row(p["g_out_a"]), tm=tm_proj, seq_len=t)

    past = past_k.shape[1]
    q3 = q.reshape(bsz, t, SB_WIDTH)
    k3 = kb.reshape(bsz, t, SB_WIDTH)
    v3 = vb.reshape(bsz, t, SB_WIDTH)
    if SB_HEADS * t == V7X_LANES and past % KEY_SUB == 0:
        by_key = lambda c: jnp.transpose(c, (0, 2, 3, 1)).reshape(bsz, SB_WIDTH, past)
        ob = _sb_attn_short(q3, by_key(past_k), by_key(past_v), k3, v3, _suffix_matrix())
    else:
        t_keys = -(-(past + t) // KEY_TILE) * KEY_TILE
        fill = jnp.zeros((bsz, t_keys - past - t, SB_WIDTH), bf16)
        cat = lambda old, new: jnp.concatenate(
            [old.reshape(bsz, past, SB_WIDTH).astype(bf16), new, fill], axis=1)
        ob = _sb_attn(q3, cat(past_k, k3), cat(past_v, v3), _suffix_matrix(), past=past, tq=tq)

    if conv_hist is None:
        hist_taps = None
    else:
        z = jnp.zeros((bsz, t - 1, 2 * D_FF), jnp.float32)
        tap1 = jnp.concatenate([conv_hist[:, 1:], z], axis=1)
        tap2 = jnp.concatenate([conv_hist, z[:, 1:]], axis=1)
        hist_taps = (tap1.reshape(m, 2 * D_FF), tap2.reshape(m, 2 * D_FF))
    y, extra = _mix_ffn(
        x.reshape(m, D_MODEL), an, ob.reshape(m, SB_WIDTH), hist_taps, row(p["g_out_b"]),
        p["w_out"].astype(bf16), row(p["g_post_mix"]), row(p["g_pre_ffn"]), p["w_up"].astype(bf16),
        p["conv_w"], row(p["conv_b"]), p["w_down"].astype(bf16), row(p["g_post_ffn"]),
        tm=tm_ffn, seq_len=t)
    if conv_hist is None:
        new_hist = extra
    else:
        new_hist = extra.reshape(bsz, t, 2 * D_FF)[:, t - (CONV_WIDTH - 1):]
    if k.ndim == 3:
        heads_last = lambda a: jnp.transpose(a.reshape(bsz, SB_HEADS, SB_HEAD_DIM, t), (0, 3, 1, 2))
    else:
        heads_last = lambda a: a.reshape(bsz, t, SB_HEADS, SB_HEAD_DIM)
    return (y.reshape(bsz, t, D_MODEL), heads_last(k), heads_last(v),
            va.reshape(bsz, t, SGU_WIDTH), new_hist)


def kernel(x_prompt, x_sample, cache_sb_k, cache_sb_v, cache_ffn_conv, w_in, g_pre_mix, ln_v_g, ln_v_b, w_spatial, b_spatial, g_out_a, g_out_b, w_out, g_post_mix, g_pre_ffn, w_up, conv_w, conv_b, w_down, g_post_ffn):
    names = ("w_in", "g_pre_mix", "ln_v_g", "ln_v_b", "w_spatial", "b_spatial", "g_out_a", "g_out_b",
             "w_out", "g_post_mix", "g_pre_ffn", "w_up", "conv_w", "conv_b", "w_down", "g_post_ffn")
    stacked = (w_in, g_pre_mix, ln_v_g, ln_v_b, w_spatial, b_spatial, g_out_a, g_out_b, w_out,
               g_post_mix, g_pre_ffn, w_up, conv_w, conv_b, w_down, g_post_ffn)
    depth = w_in.shape[0]
    yp, ys = x_prompt, x_sample
    bp = x_prompt.shape[0]
    empty_kv = jnp.zeros((bp, 0, SB_HEADS, SB_HEAD_DIM), x_prompt.dtype)
    outs = [[] for _ in range(7)]
    for l in range(depth):
        p = {n: a[l] for n, a in zip(names, stacked)}
        yp, kp, vp, _, cp = _layer(yp, empty_kv, empty_kv, None, p, tm_proj=512, tq=1024, tm_ffn=256)
        ys, ksm, vsm, vas, cs = _layer(ys, cache_sb_k[l], cache_sb_v[l], cache_ffn_conv[l], p,
                                       tm_proj=512, tq=ys.shape[1], tm_ffn=128)
        for lst, a in zip(outs, (kp, vp, ksm, vsm, vas, cp, cs)):
            lst.append(a)
    return (yp, ys) + tuple(jnp.stack(lst) for lst in outs)
```
